```python
import math
import jax, jax.numpy as jnp
from jax import lax
import numpy as np

D_MODEL = 1024
BATCH = 32
SEQ = 2048
DEPTH = 1

CHUNK = 64
Q_BLOCK = 128
N_HEADS_A = 8
NOPE_DIM = 64
ROPE_DIM = 32
V_DIM_A = 64
Q_LORA = 384
KV_LORA = 256
N_HEADS_B = 8
HEAD_DIM_B = 64
WIDTH_A = N_HEADS_A * V_DIM_A
WIDTH_B = N_HEADS_B * HEAD_DIM_B
ROPE_THETA = 10000.0
EPS = 1e-6
SPLITS = (Q_LORA, KV_LORA, ROPE_DIM, WIDTH_A, WIDTH_B, WIDTH_B, WIDTH_B,
          N_HEADS_B, WIDTH_B, D_MODEL, D_MODEL)
D_IN = sum(SPLITS)

kernel_name = "hybrid_mla_fox_gated_block"


def rms_norm(x, w):
    xf = x.astype(jnp.float32)
    y = xf * lax.rsqrt(jnp.mean(xf * xf, axis=-1, keepdims=True) + EPS)
    return (y * w.astype(jnp.float32)).astype(x.dtype)


def rope_angles(positions):
    inv_freq = ROPE_THETA ** (-jnp.arange(0, ROPE_DIM, 2, dtype=jnp.float32) / ROPE_DIM)
    ang = positions.astype(jnp.float32)[..., None] * inv_freq
    return jnp.cos(ang), jnp.sin(ang)


def apply_rope(x, cos, sin):
    x1, x2 = jnp.split(x.astype(jnp.float32), 2, axis=-1)
    y = jnp.concatenate([x1 * cos - x2 * sin, x2 * cos + x1 * sin], axis=-1)
    return y.astype(x.dtype)


def chunk_causal_mask(lo, hi):
    qp = lo + jnp.arange(Q_BLOCK)
    kp = jnp.arange(hi)
    return (kp[None, :] // CHUNK) <= (qp[:, None] // CHUNK)


def mla_attention(q_nope, q_rope, k_nope, k_rope, v):
    seq = q_nope.shape[1]
    scale = 1.0 / math.sqrt(NOPE_DIM + ROPE_DIM)
    outs = []
    for i in range(seq // Q_BLOCK):
        lo, hi = i * Q_BLOCK, (i + 1) * Q_BLOCK
        s = (jnp.einsum('bqhd,bkhd->bhqk', q_nope[:, lo:hi], k_nope[:, :hi],
                        preferred_element_type=jnp.float32)
             + jnp.einsum('bqhr,bkr->bhqk', q_rope[:, lo:hi], k_rope[:, :hi],
                          preferred_element_type=jnp.float32)) * scale
        s = jnp.where(chunk_causal_mask(lo, hi), s, -jnp.inf)
        p = jax.nn.softmax(s, axis=-1).astype(v.dtype)
        outs.append(jnp.einsum('bhqk,bkhd->bqhd', p, v[:, :hi]))
    return jnp.concatenate(outs, axis=1)


def fox_attention(q, k, v, log_f):
    seq = q.shape[1]
    scale = 1.0 / math.sqrt(HEAD_DIM_B)
    cum = jnp.swapaxes(jnp.cumsum(log_f, axis=1), 1, 2)
    outs = []
    for i in range(seq // Q_BLOCK):
        lo, hi = i * Q_BLOCK, (i + 1) * Q_BLOCK
        s = jnp.einsum('bqhd,bkhd->bhqk', q[:, lo:hi], k[:, :hi],
                       preferred_element_type=jnp.float32) * scale
        s = s + (cum[:, :, lo:hi, None] - cum[:, :, None, :hi])
        qp = lo + jnp.arange(Q_BLOCK)
        kp = jnp.arange(hi)
        s = jnp.where(kp[None, :] <= qp[:, None], s, -jnp.inf)
        p = jax.nn.softmax(s, axis=-1).astype(v.dtype)
        outs.append(jnp.einsum('bhqk,bkhd->bqhd', p, v[:, :hi]))
    return jnp.concatenate(outs, axis=1)


def setup_inputs(seed: int = 0) -> dict:
    key = jax.random.key(seed)
    ks = jax.random.split(key, 24)
    f32 = jnp.float32
    nrm = lambda k, shape, s: jax.random.normal(k, shape, f32) * s
    gain = lambda k, n: 1.0 + 0.05 * jax.random.normal(k, (DEPTH, n), f32)
    x = jax.random.normal(ks[0], (BATCH, SEQ, D_MODEL), f32)
    c = jax.random.normal(ks[1], (BATCH, D_MODEL), f32)
    offset = jax.random.randint(ks[2], (BATCH, 1), 0, 4096, dtype=jnp.int32)
    positions = offset + jnp.arange(SEQ, dtype=jnp.int32)[None, :]
    return {
        "x": x,
        "c": c,
        "positions": positions,
        "w_ada": nrm(ks[3], (DEPTH, D_MODEL, 3 * D_MODEL), 0.5 * D_MODEL ** -0.5),
        "b_ada": nrm(ks[4], (DEPTH, 3 * D_MODEL), 0.02),
        "norm_w": gain(ks[5], D_MODEL),
        "w_in": nrm(ks[6], (DEPTH, D_MODEL, D_IN), D_MODEL ** -0.5),
        "b_f": jax.random.uniform(ks[7], (DEPTH, N_HEADS_B), f32, 1.0, 5.0),
        "q_lora_norm_w": gain(ks[8], Q_LORA),
        "kv_lora_norm_w": gain(ks[9], KV_LORA),
        "w_uq": nrm(ks[10], (DEPTH, Q_LORA, N_HEADS_A * (NOPE_DIM + ROPE_DIM)), Q_LORA ** -0.5),
        "w_ukv": nrm(ks[11], (DEPTH, KV_LORA, N_HEADS_A * (NOPE_DIM + V_DIM_A)), KV_LORA ** -0.5),
        "qn_nope_a": gain(ks[12], NOPE_DIM),
        "qn_rope_a": gain(ks[13], ROPE_DIM),
        "kn_nope_a": gain(ks[14], NOPE_DIM),
        "kn_rope_a": gain(ks[15], ROPE_DIM),
        "qn_b": gain(ks[16], HEAD_DIM_B),
        "kn_b": gain(ks[17], HEAD_DIM_B),
        "w_branch_a": nrm(ks[18], (DEPTH, WIDTH_A, D_MODEL), WIDTH_A ** -0.5),
        "w_branch_b": nrm(ks[19], (DEPTH, WIDTH_B, D_MODEL), WIDTH_B ** -0.5),
        "w_out": nrm(ks[20], (DEPTH, D_MODEL, D_MODEL), D_MODEL ** -0.5),
    }


def reference(x, c, positions, w_ada, b_ada, norm_w, w_in, b_f, q_lora_norm_w,
              kv_lora_norm_w, w_uq, w_ukv, qn_nope_a, qn_rope_a, kn_nope_a,
              kn_rope_a, qn_b, kn_b, w_branch_a, w_branch_b, w_out):
    B, S, _ = x.shape
    cos, sin = rope_angles(positions)
    split_points = np.cumsum(SPLITS)[:-1].tolist()
    for l in range(DEPTH):
        ada = c @ w_ada[l] + b_ada[l]
        shift, scale, gate = jnp.split(ada, 3, axis=-1)
        h = rms_norm(x, norm_w[l]) * (1.0 + scale[:, None, :]) + shift[:, None, :]

        proj = h @ w_in[l]
        (cq, ckv, k_rope, gate_a, q_b, k_b, v_b, f_b, gate_b,
         mg_a, mg_b) = jnp.split(proj, split_points, axis=-1)

        q = (rms_norm(cq, q_lora_norm_w[l]) @ w_uq[l]).reshape(B, S, N_HEADS_A, NOPE_DIM + ROPE_DIM)
        q_nope, q_rope = q[..., :NOPE_DIM], q[..., NOPE_DIM:]
        kv = (rms_norm(ckv, kv_lora_norm_w[l]) @ w_ukv[l]).reshape(B, S, N_HEADS_A, NOPE_DIM + V_DIM_A)
        k_nope, v_a = kv[..., :NOPE_DIM], kv[..., NOPE_DIM:]
        q_nope = rms_norm(q_nope, qn_nope_a[l])
        q_rope = apply_rope(rms_norm(q_rope, qn_rope_a[l]), cos[:, :, None, :], sin[:, :, None, :])
        k_nope = rms_norm(k_nope, kn_nope_a[l])
        k_rope = apply_rope(rms_norm(k_rope, kn_rope_a[l]), cos, sin)
        o_a = mla_attention(q_nope, q_rope, k_nope, k_rope, v_a).reshape(B, S, WIDTH_A)
        y_a = o_a * jax.nn.silu(gate_a)

        qb = rms_norm(q_b.reshape(B, S, N_HEADS_B, HEAD_DIM_B), qn_b[l])
        kb = rms_norm(k_b.reshape(B, S, N_HEADS_B, HEAD_DIM_B), kn_b[l])
        vb = v_b.reshape(B, S, N_HEADS_B, HEAD_DIM_B)
        log_f = jax.nn.log_sigmoid(f_b.astype(jnp.float32) + b_f[l].astype(jnp.float32))
        o_b = fox_attention(qb, kb, vb, log_f).reshape(B, S, WIDTH_B)
        y_b = o_b * jax.nn.silu(gate_b)

        merged = (jax.nn.sigmoid(mg_a) * (y_a @ w_branch_a[l])
                  + jax.nn.sigmoid(mg_b) * (y_b @ w_branch_b[l]))
        x = x + gate[:, None, :] * (merged @ w_out[l])
    return x
```

```python
import functools
import math

import jax
import jax.numpy as jnp
from jax import lax
from jax.experimental import pallas as pl
from jax.experimental.pallas import tpu as pltpu

D_MODEL = 1024
CHUNK = 64
N_HEADS = 8
NOPE_DIM = 64
ROPE_DIM = 32
V_DIM = 64
Q_LORA = 384
KV_LORA = 256
WIDTH = N_HEADS * V_DIM
ROPE_THETA = 10000.0
EPS = 1e-6

LANES = 128
HEAD_PAD = 128
MXU_DIM = 256
VMEM_LIMIT = 48 * 1024 * 1024

C_CQ = 0
C_CKV = C_CQ + Q_LORA
C_KR = C_CKV + KV_LORA
C_QB = C_KR + LANES
C_KB = C_QB + WIDTH
C_VB = C_KB + WIDTH
C_GA = C_VB + WIDTH
C_GB = C_GA + WIDTH
C_MA = C_GB + WIDTH
C_MB = C_MA + D_MODEL
C_END = C_MB + D_MODEL
F_LANE0 = 0
ROPE_LANE0 = NOPE_DIM

BF16 = jnp.bfloat16
F32 = jnp.float32


def _const_spec(shape):
    nd = len(shape)
    return pl.BlockSpec(shape, lambda *_: (0,) * nd, pipeline_mode=pl.Buffered(1))


def _ada_kernel(c_ref, w_ref, b_ref, o_ref):
    o_ref[...] = jnp.dot(c_ref[...].astype(BF16), w_ref[...].astype(BF16),
                         preferred_element_type=F32) + b_ref[...]


def _ada_proj(c, w_ada, b_ada):
    bsz, d = c.shape
    n = w_ada.shape[1]
    tn = 512
    return pl.pallas_call(
        _ada_kernel,
        grid=(n // tn,),
        in_specs=[pl.BlockSpec((bsz, d), lambda j: (0, 0)),
                  pl.BlockSpec((d, tn), lambda j: (0, j)),
                  pl.BlockSpec((1, tn), lambda j: (0, j))],
        out_specs=pl.BlockSpec((bsz, tn), lambda j: (0, j)),
        out_shape=jax.ShapeDtypeStruct((bsz, n), F32),
        name="ada_proj",
    )(c, w_ada, b_ada.reshape(1, n))


def _rms(x, n):
    return lax.rsqrt(jnp.sum(x * x, axis=-1, keepdims=True) * (1.0 / n) + EPS)


def _group_mean_sq(x, g_ref):
    g = g_ref[...]
    sq = (x * x).astype(BF16)
    parts = [jnp.dot(sq[:, c:c + MXU_DIM], g, preferred_element_type=F32)
             for c in range(0, x.shape[1], MXU_DIM)]
    return jnp.concatenate(parts, axis=1)


def _rot_half(x):
    lane = lax.broadcasted_iota(jnp.int32, x.shape, 1)
    half = ROPE_DIM // 2
    return jnp.where(lane < ROPE_LANE0 + half,
                     pltpu.roll(x, LANES - half, axis=1),
                     pltpu.roll(x, half, axis=1))


def _in_proj_kernel(x_ref, ada_ref, pos_ref, nw_ref, win_ref, wuq_ref, wuk_ref,
                    wuv_ref, ga_mat_ref, gb_mat_ref, qlw_ref, kvlw_ref, qaw_ref,
                    kaw_ref, krw_ref, qbw_ref, kbw_ref, invf_ref,
                    qa_ref, ka_ref, va_ref, qb_ref, kb_ref, vb_ref, f_ref,
                    ga_ref, gb_ref, ma_ref, mb_ref):
    x = x_ref[0]
    shift = ada_ref[0, 0:1, :]
    scale = ada_ref[0, 1:2, :]
    h = (x * _rms(x, D_MODEL)) * nw_ref[...] * (1.0 + scale) + shift
    hb = h.astype(BF16)

    def proj(lo, hi):
        return jnp.dot(hb, win_ref[:, lo:hi], preferred_element_type=F32)

    ang = pos_ref[0].astype(F32) * invf_ref[...]
    lane = lax.broadcasted_iota(jnp.int32, ang.shape, 1)
    cos_t = jnp.cos(ang)
    sin_t = jnp.sin(ang)
    sin_t = jnp.where(lane < ROPE_LANE0 + ROPE_DIM // 2, -sin_t, sin_t)

    def rope(t):
        return t * cos_t + _rot_half(t) * sin_t

    lat = proj(C_CQ, C_QB)
    cq = lat[:, C_CQ:C_CKV]
    ckv = lat[:, C_CKV:C_KR]
    kr = lat[:, C_KR:C_QB]

    cqn = (cq * _rms(cq, Q_LORA) * qlw_ref[...]).astype(BF16)
    q = jnp.dot(cqn, wuq_ref[...], preferred_element_type=F32)
    qn = q * lax.rsqrt(_group_mean_sq(q, ga_mat_ref) + EPS) * qaw_ref[...]
    for hd in range(N_HEADS):
        sl = slice(hd * HEAD_PAD, (hd + 1) * HEAD_PAD)
        qa_ref[0, :, sl] = rope(qn[:, sl]).astype(BF16)

    ckvn = (ckv * _rms(ckv, KV_LORA) * kvlw_ref[...]).astype(BF16)
    k = jnp.dot(ckvn, wuk_ref[...], preferred_element_type=F32)
    kn = k * lax.rsqrt(_group_mean_sq(k, ga_mat_ref) + EPS) * kaw_ref[...]
    rope_lane = (lane >= ROPE_LANE0) & (lane < ROPE_LANE0 + ROPE_DIM)
    kr_ms = jnp.sum(jnp.where(rope_lane, kr * kr, 0.0), axis=-1, keepdims=True)
    krn = kr * lax.rsqrt(kr_ms * (1.0 / ROPE_DIM) + EPS) * krw_ref[...]
    krr = rope(krn)
    for hd in range(N_HEADS):
        sl = slice(hd * HEAD_PAD, (hd + 1) * HEAD_PAD)
        ka_ref[0, :, sl] = (kn[:, sl] + krr).astype(BF16)
    va_ref[0] = jnp.dot(ckvn, wuv_ref[...], preferred_element_type=F32).astype(BF16)
    f_ref[0] = kr[:, F_LANE0:F_LANE0 + N_HEADS]

    qb = proj(C_QB, C_KB)
    qb_ref[0] = (qb * lax.rsqrt(_group_mean_sq(qb, gb_mat_ref) + EPS)
                 * qbw_ref[...]).astype(BF16)
    kb = proj(C_KB, C_VB)
    kb_ref[0] = (kb * lax.rsqrt(_group_mean_sq(kb, gb_mat_ref) + EPS)
                 * kbw_ref[...]).astype(BF16)
    vb_ref[0] = proj(C_VB, C_GA).astype(BF16)

    ga_ref[0] = proj(C_GA, C_GB).astype(BF16)
    gb_ref[0] = proj(C_GB, C_MA).astype(BF16)
    ma_ref[0] = proj(C_MA, C_MB).astype(BF16)
    mb_ref[0] = proj(C_MB, C_END).astype(BF16)


def _in_proj(x, ada3, pos3, consts, tm):
    bsz, seq, d = x.shape
    grid = (bsz, seq // tm)
    tok = lambda width: pl.BlockSpec((1, tm, width), lambda b, i: (b, i, 0))
    in_specs = [tok(d),
                pl.BlockSpec((1, 3, d), lambda b, i: (b, 0, 0)),
                tok(1)] + [_const_spec(a.shape) for a in consts]
    out_widths = [N_HEADS * HEAD_PAD, N_HEADS * HEAD_PAD, WIDTH, WIDTH, WIDTH, WIDTH]
    out_shape = [jax.ShapeDtypeStruct((bsz, seq, w), BF16) for w in out_widths]
    out_specs = [tok(w) for w in out_widths]
    out_shape.append(jax.ShapeDtypeStruct((bsz, seq, N_HEADS), F32))
    out_specs.append(tok(N_HEADS))
    for w in (WIDTH, WIDTH, D_MODEL, D_MODEL):
        out_shape.append(jax.ShapeDtypeStruct((bsz, seq, w), BF16))
        out_specs.append(tok(w))
    return pl.pallas_call(
        _in_proj_kernel,
        grid=grid,
        in_specs=in_specs,
        out_specs=out_specs,
        out_shape=out_shape,
        compiler_params=pltpu.CompilerParams(
            dimension_semantics=("parallel", "arbitrary"),
            vmem_limit_bytes=VMEM_LIMIT),
        name="in_proj",
    )(x, ada3, pos3, *consts)


def _cumsum_kernel(f_ref, b_ref, o_ref):
    z = f_ref[...] + b_ref[...]
    x = jnp.minimum(z, 0.0) - jnp.log1p(jnp.exp(-jnp.abs(z)))
    lane = lax.broadcasted_iota(jnp.int32, x.shape, 1)
    step = 1
    while step < x.shape[1]:
        x = x + jnp.where(lane >= step, pltpu.roll(x, step, axis=1), 0.0)
        step *= 2
    o_ref[...] = x


def _fox_cumsum(f_t, b_rows):
    rows, seq = f_t.shape
    tr = 64
    return pl.pallas_call(
        _cumsum_kernel,
        grid=(rows // tr,),
        in_specs=[pl.BlockSpec((tr, seq), lambda r: (r, 0)),
                  pl.BlockSpec((tr, 1), lambda r: (r, 0))],
        out_specs=pl.BlockSpec((tr, seq), lambda r: (r, 0)),
        out_shape=jax.ShapeDtypeStruct((rows, seq), F32),
        name="fox_cumsum",
    )(f_t, b_rows)


def _attn_kernel(*refs, tq, head_lanes, chunk, use_bias):
    if use_bias:
        q_ref, k_ref, v_ref, g_ref, c_ref, o_ref = refs
    else:
        q_ref, k_ref, v_ref, g_ref, o_ref = refs
        c_ref = None
    qi = pl.program_id(2)
    lane_q = lax.broadcasted_iota(jnp.int32, (tq, LANES), 1)
    row = lax.broadcasted_iota(jnp.int32, (tq, tq), 0)
    col = lax.broadcasted_iota(jnp.int32, (tq, tq), 1)
    diag_mask = (col // chunk) <= (row // chunk)
    nt = (((1,), (1,)), ((), ()))

    outs = []
    for hh in range(2):
        if head_lanes == LANES:
            sl = slice(hh * LANES, (hh + 1) * LANES)
            q = q_ref[0, :, sl]
        else:
            sl = slice(0, LANES)
            qp = q_ref[0]
            keep = (lane_q < V_DIM) if hh == 0 else (lane_q >= V_DIM)
            q = jnp.where(keep, qp, jnp.zeros_like(qp))

        def step(off, carry, masked):
            m, l, acc = carry
            k = k_ref[0, pl.ds(off, tq), sl]
            v = v_ref[0, pl.ds(off, tq), :]
            s = lax.dot_general(q, k, nt, preferred_element_type=F32)
            if use_bias:
                s = s - c_ref[0, 0, hh:hh + 1, pl.ds(off, tq)]
            if masked:
                s = jnp.where(diag_mask, s, -jnp.inf)
            m_new = jnp.maximum(m, jnp.max(s, axis=-1, keepdims=True))
            alpha = jnp.exp(m - m_new)
            p = jnp.exp(s - m_new)
            l = alpha * l + jnp.sum(p, axis=-1, keepdims=True)
            acc = alpha * acc + jnp.dot(p.astype(BF16), v, preferred_element_type=F32)
            return m_new, l, acc

        init = (jnp.full((tq, 1), -jnp.inf, F32), jnp.zeros((tq, 1), F32),
                jnp.zeros((tq, LANES), F32))
        carry = lax.fori_loop(
            0, qi, lambda j, c: step(pl.multiple_of(j * tq, tq), c, False), init)
        _, l, acc = step(pl.multiple_of(qi * tq, tq), carry, True)
        outs.append(acc / l)

    o = jnp.where(lane_q < V_DIM, outs[0], outs[1])
    g = g_ref[0].astype(F32)
    o_ref[0] = (o * (g * jax.nn.sigmoid(g))).astype(BF16)


def _attention(q, k, v, gate, cum, *, tq, head_lanes, chunk):
    bsz, seq, _ = v.shape
    pairs = N_HEADS // 2
    qk_w = 2 * head_lanes if head_lanes == LANES else LANES
    grid = (bsz, pairs, seq // tq)
    in_specs = [pl.BlockSpec((1, tq, qk_w), lambda b, p, i: (b, i, p)),
                pl.BlockSpec((1, seq, qk_w), lambda b, p, i: (b, 0, p)),
                pl.BlockSpec((1, seq, LANES), lambda b, p, i: (b, 0, p)),
                pl.BlockSpec((1, tq, LANES), lambda b, p, i: (b, i, p))]
    args = [q, k, v, gate]
    if cum is not None:
        in_specs.append(pl.BlockSpec((1, 1, 2, seq), lambda b, p, i: (b, p, 0, 0)))
        args.append(cum)
    kern = functools.partial(_attn_kernel, tq=tq, head_lanes=head_lanes,
                             chunk=chunk, use_bias=cum is not None)
    return pl.pallas_call(
        kern,
        grid=grid,
        in_specs=in_specs,
        out_specs=pl.BlockSpec((1, tq, LANES), lambda b, p, i: (b, i, p)),
        out_shape=jax.ShapeDtypeStruct((bsz, seq, WIDTH), BF16),
        compiler_params=pltpu.CompilerParams(
            dimension_semantics=("parallel", "parallel", "arbitrary"),
            vmem_limit_bytes=VMEM_LIMIT),
        name="fox_attn" if cum is not None else "mla_attn",
    )(*args)


def _out_kernel(x_ref, ada_ref, ya_ref, yb_ref, ma_ref, mb_ref, wa_ref, wb_ref,
                wo_ref, o_ref):
    pa = jnp.dot(ya_ref[0], wa_ref[...], preferred_element_type=F32)
    pb = jnp.dot(yb_ref[0], wb_ref[...], preferred_element_type=F32)
    merged = (jax.nn.sigmoid(ma_ref[0].astype(F32)) * pa
              + jax.nn.sigmoid(mb_ref[0].astype(F32)) * pb)
    upd = jnp.dot(merged.astype(BF16), wo_ref[...], preferred_element_type=F32)
    o_ref[0] = x_ref[0] + ada_ref[0, 2:3, :] * upd


def _out_proj(x, ada3, ya, yb, ma, mb, wa, wb, wo, tm):
    bsz, seq, d = x.shape
    tok = lambda width: pl.BlockSpec((1, tm, width), lambda b, i: (b, i, 0))
    return pl.pallas_call(
        _out_kernel,
        grid=(bsz, seq // tm),
        in_specs=[tok(d), pl.BlockSpec((1, 3, d), lambda b, i: (b, 0, 0)),
                  tok(WIDTH), tok(WIDTH), tok(d), tok(d),
                  _const_spec(wa.shape), _const_spec(wb.shape), _const_spec(wo.shape)],
        out_specs=tok(d),
        out_shape=jax.ShapeDtypeStruct((bsz, seq, d), x.dtype),
        compiler_params=pltpu.CompilerParams(
            dimension_semantics=("parallel", "arbitrary"),
            vmem_limit_bytes=VMEM_LIMIT),
        name="out_proj",
    )(x, ada3, ya, yb, ma, mb, wa, wb, wo)


def _tile_heads(vec, pad_to):
    v = jnp.pad(vec.astype(F32), (0, pad_to - vec.shape[0]))
    return jnp.tile(v, N_HEADS).reshape(1, N_HEADS * pad_to)


def _block_avg(sizes, total):
    idx = jnp.arange(total)
    gid = jnp.full((total,), -1, jnp.int32)
    inv = jnp.zeros((total,), F32)
    start = 0
    for n, (size, live) in enumerate(sizes):
        inside = (idx >= start) & (idx < start + size)
        if live:
            gid = jnp.where(inside, n, gid)
            inv = jnp.where(inside, 1.0 / size, inv)
        start += size
    same = (gid[:, None] == gid[None, :]) & (gid[:, None] >= 0)
    return jnp.where(same, inv[None, :], 0.0).astype(BF16)


def kernel(x, c, positions, w_ada, b_ada, norm_w, w_in, b_f, q_lora_norm_w,
           kv_lora_norm_w, w_uq, w_ukv, qn_nope_a, qn_rope_a, kn_nope_a,
           kn_rope_a, qn_b, kn_b, w_branch_a, w_branch_b, w_out):
    bsz, seq, d = x.shape
    depth = w_in.shape[0]
    tm = 512
    tq = 256
    pad = HEAD_PAD - NOPE_DIM - ROPE_DIM

    inv_freq = ROPE_THETA ** (-jnp.arange(0, ROPE_DIM, 2, dtype=F32) / ROPE_DIM)
    invf = jnp.concatenate([jnp.zeros((ROPE_LANE0,), F32), inv_freq, inv_freq,
                            jnp.zeros((pad,), F32)]).reshape(1, LANES)
    pos3 = positions.reshape(bsz, seq, 1)
    head = 2 * [(NOPE_DIM, True), (ROPE_DIM, True), (pad, False)]
    g_a = _block_avg(head, MXU_DIM)
    g_b = _block_avg(4 * [(V_DIM, True)], MXU_DIM)
    sm_a = 1.0 / math.sqrt(NOPE_DIM + ROPE_DIM)
    sm_b = 1.0 / math.sqrt(V_DIM)

    for l in range(depth):
        ada3 = _ada_proj(c, w_ada[l], b_ada[l]).reshape(bsz, 3, d)

        wi = w_in[l]
        o = [0]
        for s in (Q_LORA, KV_LORA, ROPE_DIM, WIDTH, WIDTH, WIDTH, WIDTH, N_HEADS,
                  WIDTH, D_MODEL, D_MODEL):
            o.append(o[-1] + s)
        (w_cq, w_ckv, w_kr, w_ga, w_qb, w_kb, w_vb, w_f, w_gb, w_ma,
         w_mb) = [wi[:, o[n]:o[n + 1]] for n in range(11)]
        w_krf = jnp.concatenate(
            [w_f, jnp.zeros((d, ROPE_LANE0 - N_HEADS), F32), w_kr,
             jnp.zeros((d, pad), F32)], axis=1)
        win = jnp.concatenate([w_cq, w_ckv, w_krf, w_qb, w_kb, w_vb, w_ga, w_gb,
                               w_ma, w_mb], axis=1).astype(BF16)

        wuq = jnp.pad(w_uq[l].reshape(Q_LORA, N_HEADS, NOPE_DIM + ROPE_DIM),
                      ((0, 0), (0, 0), (0, pad))
                      ).reshape(Q_LORA, N_HEADS * HEAD_PAD).astype(BF16)
        wkv = w_ukv[l].reshape(KV_LORA, N_HEADS, NOPE_DIM + V_DIM)
        wuk = jnp.pad(wkv[:, :, :NOPE_DIM], ((0, 0), (0, 0), (0, HEAD_PAD - NOPE_DIM))
                      ).reshape(KV_LORA, N_HEADS * HEAD_PAD).astype(BF16)
        wuv = wkv[:, :, NOPE_DIM:].reshape(KV_LORA, WIDTH).astype(BF16)

        qaw = _tile_heads(jnp.concatenate([qn_nope_a[l], qn_rope_a[l]]) * sm_a, HEAD_PAD)
        kaw = _tile_heads(kn_nope_a[l], HEAD_PAD)
        krw = jnp.concatenate([jnp.zeros((ROPE_LANE0,), F32), kn_rope_a[l],
                               jnp.zeros((pad,), F32)]).reshape(1, LANES)
        qbw = _tile_heads(qn_b[l] * sm_b, V_DIM)
        kbw = _tile_heads(kn_b[l], V_DIM)

        consts = [norm_w[l].reshape(1, d), win, wuq, wuk, wuv, g_a, g_b,
                  q_lora_norm_w[l].reshape(1, Q_LORA),
                  kv_lora_norm_w[l].reshape(1, KV_LORA), qaw, kaw, krw, qbw, kbw, invf]
        (qa, ka, va, qb, kb, vb, f, ga, gb, ma, mb) = _in_proj(x, ada3, pos3, consts, tm)

        f_t = jnp.swapaxes(f, 1, 2).reshape(bsz * N_HEADS, seq)
        b_rows = jnp.tile(b_f[l].astype(F32), bsz).reshape(bsz * N_HEADS, 1)
        cum = _fox_cumsum(f_t, b_rows).reshape(bsz, N_HEADS // 2, 2, seq)

        ya = _attention(qa, ka, va, ga, None, tq=tq, head_lanes=HEAD_PAD, chunk=CHUNK)
        yb = _attention(qb, kb, vb, gb, cum, tq=tq, head_lanes=V_DIM, chunk=1)

        x = _out_proj(x, ada3, ya, yb, ma, mb, w_branch_a[l].astype(BF16),
                      w_branch_b[l].astype(BF16), w_out[l].astype(BF16), tm)
    return x
```

```python
import functools
import math

import jax
import jax.numpy as jnp
from jax import lax
from jax.experimental import pallas as pl
from jax.experimental.pallas import tpu as pltpu

D_MODEL = 1024
CHUNK = 64
N_HEADS = 8
NOPE_DIM = 64
ROPE_DIM = 32
V_DIM = 64
Q_LORA = 384
KV_LORA = 256
WIDTH = N_HEADS * V_DIM
ROPE_THETA = 10000.0
EPS = 1e-6

LANES = 128
HEAD_PAD = 128
MXU_DIM = 256
VMEM_LIMIT = 48 * 1024 * 1024

C_CQ = 0
C_CKV = C_CQ + Q_LORA
C_KR = C_CKV + KV_LORA
C_QB = C_KR + LANES
C_KB = C_QB + WIDTH
C_VB = C_KB + WIDTH
C_GA = C_VB + WIDTH
C_GB = C_GA + WIDTH
C_MA = C_GB + WIDTH
C_MB = C_MA + D_MODEL
C_END = C_MB + D_MODEL
F_LANE0 = 0
ROPE_LANE0 = NOPE_DIM

BF16 = jnp.bfloat16
F32 = jnp.float32


def _const_spec(shape):
    nd = len(shape)
    return pl.BlockSpec(shape, lambda *_: (0,) * nd, pipeline_mode=pl.Buffered(1))


def _ada_kernel(c_ref, w_ref, b_ref, o_ref):
    o_ref[...] = jnp.dot(c_ref[...].astype(BF16), w_ref[...].astype(BF16),
                         preferred_element_type=F32) + b_ref[...]


def _ada_proj(c, w_ada, b_ada):
    bsz, d = c.shape
    n = w_ada.shape[1]
    tn = 512
    return pl.pallas_call(
        _ada_kernel,
        grid=(n // tn,),
        in_specs=[pl.BlockSpec((bsz, d), lambda j: (0, 0)),
                  pl.BlockSpec((d, tn), lambda j: (0, j)),
                  pl.BlockSpec((1, tn), lambda j: (0, j))],
        out_specs=pl.BlockSpec((bsz, tn), lambda j: (0, j)),
        out_shape=jax.ShapeDtypeStruct((bsz, n), F32),
        name="ada_proj",
    )(c, w_ada, b_ada.reshape(1, n))


def _rms(x, n):
    return lax.rsqrt(jnp.sum(x * x, axis=-1, keepdims=True) * (1.0 / n) + EPS)


def _group_mean_sq(x, g_ref):
    g = g_ref[...]
    sq = (x * x).astype(BF16)
    parts = [jnp.dot(sq[:, c:c + MXU_DIM], g, preferred_element_type=F32)
             for c in range(0, x.shape[1], MXU_DIM)]
    return jnp.concatenate(parts, axis=1)


def _rot_half(x):
    lane = lax.broadcasted_iota(jnp.int32, x.shape, 1)
    half = ROPE_DIM // 2
    return jnp.where(lane < ROPE_LANE0 + half,
                     pltpu.roll(x, LANES - half, axis=1),
                     pltpu.roll(x, half, axis=1))


def _in_proj_kernel(x_ref, ada_ref, pos_ref, nw_ref, win_ref, wuq_ref, wuk_ref,
                    wuv_ref, ga_mat_ref, gb_mat_ref, qlw_ref, kvlw_ref, qaw_ref,
                    kaw_ref, krw_ref, qbw_ref, kbw_ref, invf_ref,
                    qa_ref, ka_ref, va_ref, qb_ref, kb_ref, vb_ref, f_ref,
                    ga_ref, gb_ref, ma_ref, mb_ref):
    x = x_ref[0]
    shift = ada_ref[0, 0:1, :]
    scale = ada_ref[0, 1:2, :]
    h = (x * _rms(x, D_MODEL)) * nw_ref[...] * (1.0 + scale) + shift
    hb = h.astype(BF16)

    def proj(lo, hi):
        return jnp.dot(hb, win_ref[:, lo:hi], preferred_element_type=F32)

    ang = pos_ref[0].astype(F32) * invf_ref[...]
    lane = lax.broadcasted_iota(jnp.int32, ang.shape, 1)
    cos_t = jnp.cos(ang)
    sin_t = jnp.sin(ang)
    sin_t = jnp.where(lane < ROPE_LANE0 + ROPE_DIM // 2, -sin_t, sin_t)

    def rope(t):
        return t * cos_t + _rot_half(t) * sin_t

    lat = proj(C_CQ, C_QB)
    cq = lat[:, C_CQ:C_CKV]
    ckv = lat[:, C_CKV:C_KR]
    kr = lat[:, C_KR:C_QB]

    cqn = (cq * _rms(cq, Q_LORA) * qlw_ref[...]).astype(BF16)
    q = jnp.dot(cqn, wuq_ref[...], preferred_element_type=F32)
    qn = q * lax.rsqrt(_group_mean_sq(q, ga_mat_ref) + EPS) * qaw_ref[...]
    for hd in range(N_HEADS):
        sl = slice(hd * HEAD_PAD, (hd + 1) * HEAD_PAD)
        qa_ref[0, :, sl] = rope(qn[:, sl]).astype(BF16)

    ckvn = (ckv * _rms(ckv, KV_LORA) * kvlw_ref[...]).astype(BF16)
    k = jnp.dot(ckvn, wuk_ref[...], preferred_element_type=F32)
    kn = k * lax.rsqrt(_group_mean_sq(k, ga_mat_ref) + EPS) * kaw_ref[...]
    rope_lane = (lane >= ROPE_LANE0) & (lane < ROPE_LANE0 + ROPE_DIM)
    kr_ms = jnp.sum(jnp.where(rope_lane, kr * kr, 0.0), axis=-1, keepdims=True)
    krn = kr * lax.rsqrt(kr_ms * (1.0 / ROPE_DIM) + EPS) * krw_ref[...]
    krr = rope(krn)
    for hd in range(N_HEADS):
        sl = slice(hd * HEAD_PAD, (hd + 1) * HEAD_PAD)
        ka_ref[0, :, sl] = (kn[:, sl] + krr).astype(BF16)
    va_ref[0] = jnp.dot(ckvn, wuv_ref[...], preferred_element_type=F32).astype(BF16)
    f_ref[0] = kr[:, F_LANE0:F_LANE0 + N_HEADS]

    qb = proj(C_QB, C_KB)
    qb_ref[0] = (qb * lax.rsqrt(_group_mean_sq(qb, gb_mat_ref) + EPS)
                 * qbw_ref[...]).astype(BF16)
    kb = proj(C_KB, C_VB)
    kb_ref[0] = (kb * lax.rsqrt(_group_mean_sq(kb, gb_mat_ref) + EPS)
                 * kbw_ref[...]).astype(BF16)
    vb_ref[0] = proj(C_VB, C_GA).astype(BF16)

    ga_ref[0] = proj(C_GA, C_GB).astype(BF16)
    gb_ref[0] = proj(C_GB, C_MA).astype(BF16)
    ma_ref[0] = proj(C_MA, C_MB).astype(BF16)
    mb_ref[0] = proj(C_MB, C_END).astype(BF16)


def _in_proj(x, ada3, pos3, consts, tm):
    bsz, seq, d = x.shape
    grid = (bsz, seq // tm)
    tok = lambda width: pl.BlockSpec((1, tm, width), lambda b, i: (b, i, 0))
    in_specs = [tok(d),
                pl.BlockSpec((1, 3, d), lambda b, i: (b, 0, 0)),
                tok(1)] + [_const_spec(a.shape) for a in consts]
    out_widths = [N_HEADS * HEAD_PAD, N_HEADS * HEAD_PAD, WIDTH, WIDTH, WIDTH, WIDTH]
    out_shape = [jax.ShapeDtypeStruct((bsz, seq, w), BF16) for w in out_widths]
    out_specs = [tok(w) for w in out_widths]
    out_shape.append(jax.ShapeDtypeStruct((bsz, seq, N_HEADS), F32))
    out_specs.append(tok(N_HEADS))
    for w in (WIDTH, WIDTH, D_MODEL, D_MODEL):
        out_shape.append(jax.ShapeDtypeStruct((bsz, seq, w), BF16))
        out_specs.append(tok(w))
    return pl.pallas_call(
        _in_proj_kernel,
        grid=grid,
        in_specs=in_specs,
        out_specs=out_specs,
        out_shape=out_shape,
        compiler_params=pltpu.CompilerParams(
            dimension_semantics=("parallel", "arbitrary"),
            vmem_limit_bytes=VMEM_LIMIT),
        name="in_proj",
    )(x, ada3, pos3, *consts)


def _cumsum_kernel(f_ref, b_ref, o_ref):
    z = f_ref[...] + b_ref[...]
    x = jnp.minimum(z, 0.0) - jnp.log1p(jnp.exp(-jnp.abs(z)))
    lane = lax.broadcasted_iota(jnp.int32, x.shape, 1)
    step = 1
    while step < x.shape[1]:
        x = x + jnp.where(lane >= step, pltpu.roll(x, step, axis=1), 0.0)
        step *= 2
    o_ref[...] = x


def _fox_cumsum(f_t, b_rows):
    rows, seq = f_t.shape
    tr = 64
    return pl.pallas_call(
        _cumsum_kernel,
        grid=(rows // tr,),
        in_specs=[pl.BlockSpec((tr, seq), lambda r: (r, 0)),
                  pl.BlockSpec((tr, 1), lambda r: (r, 0))],
        out_specs=pl.BlockSpec((tr, seq), lambda r: (r, 0)),
        out_shape=jax.ShapeDtypeStruct((rows, seq), F32),
        name="fox_cumsum",
    )(f_t, b_rows)


def _attn_kernel(*refs, tq, head_lanes, chunk, use_bias):
    if use_bias:
        q_ref, k_ref, v_ref, g_ref, c_ref, o_ref = refs
    else:
        q_ref, k_ref, v_ref, g_ref, o_ref = refs
        c_ref = None
    qi = pl.program_id(2)
    lane_q = lax.broadcasted_iota(jnp.int32, (tq, LANES), 1)
    row = lax.broadcasted_iota(jnp.int32, (tq, tq), 0)
    col = lax.broadcasted_iota(jnp.int32, (tq, tq), 1)
    diag_mask = (col // chunk) <= (row // chunk)
    nt = (((1,), (1,)), ((), ()))

    qs, sls = [], []
    for hh in range(2):
        if head_lanes == LANES:
            sls.append(slice(hh * LANES, (hh + 1) * LANES))
            qs.append(q_ref[0, :, sls[hh]])
        else:
            sls.append(slice(0, LANES))
            qp = q_ref[0]
            keep = (lane_q < V_DIM) if hh == 0 else (lane_q >= V_DIM)
            qs.append(jnp.where(keep, qp, jnp.zeros_like(qp)))

    def step(off, carry, masked):
        v = v_ref[0, pl.ds(off, tq), :]
        out = []
        for hh in range(2):
            m, l, acc = carry[hh]
            k = k_ref[0, pl.ds(off, tq), sls[hh]]
            s = lax.dot_general(qs[hh], k, nt, preferred_element_type=F32)
            if use_bias:
                s = s - c_ref[0, 0, hh:hh + 1, pl.ds(off, tq)]
            if masked:
                s = jnp.where(diag_mask, s, -jnp.inf)
            m_new = jnp.maximum(m, jnp.max(s, axis=-1, keepdims=True))
            alpha = jnp.exp(m - m_new)
            p = jnp.exp(s - m_new)
            l = alpha * l + jnp.sum(p, axis=-1, keepdims=True)
            acc = alpha * acc + jnp.dot(p.astype(BF16), v, preferred_element_type=F32)
            out.append((m_new, l, acc))
        return tuple(out)

    init = tuple((jnp.full((tq, 1), -jnp.inf, F32), jnp.zeros((tq, 1), F32),
                  jnp.zeros((tq, LANES), F32)) for _ in range(2))
    carry = lax.fori_loop(
        0, qi, lambda j, c: step(pl.multiple_of(j * tq, tq), c, False), init)
    carry = step(pl.multiple_of(qi * tq, tq), carry, True)
    outs = [acc / l for _, l, acc in carry]

    o = jnp.where(lane_q < V_DIM, outs[0], outs[1])
    g = g_ref[0].astype(F32)
    o_ref[0] = (o * (g * jax.nn.sigmoid(g))).astype(BF16)


def _attention(q, k, v, gate, cum, *, tq, head_lanes, chunk):
    bsz, seq, _ = v.shape
    pairs = N_HEADS // 2
    qk_w = 2 * head_lanes if head_lanes == LANES else LANES
    grid = (bsz, pairs, seq // tq)
    in_specs = [pl.BlockSpec((1, tq, qk_w), lambda b, p, i: (b, i, p)),
                pl.BlockSpec((1, seq, qk_w), lambda b, p, i: (b, 0, p)),
                pl.BlockSpec((1, seq, LANES), lambda b, p, i: (b, 0, p)),
                pl.BlockSpec((1, tq, LANES), lambda b, p, i: (b, i, p))]
    args = [q, k, v, gate]
    if cum is not None:
        in_specs.append(pl.BlockSpec((1, 1, 2, seq), lambda b, p, i: (b, p, 0, 0)))
        args.append(cum)
    kern = functools.partial(_attn_kernel, tq=tq, head_lanes=head_lanes,
                             chunk=chunk, use_bias=cum is not None)
    return pl.pallas_call(
        kern,
        grid=grid,
        in_specs=in_specs,
        out_specs=pl.BlockSpec((1, tq, LANES), lambda b, p, i: (b, i, p)),
        out_shape=jax.ShapeDtypeStruct((bsz, seq, WIDTH), BF16),
        compiler_params=pltpu.CompilerParams(
            dimension_semantics=("parallel", "parallel", "arbitrary"),
            vmem_limit_bytes=VMEM_LIMIT),
        name="fox_attn" if cum is not None else "mla_attn",
    )(*args)


def _out_kernel(x_ref, ada_ref, ya_ref, yb_ref, ma_ref, mb_ref, wa_ref, wb_ref,
                wo_ref, o_ref):
    pa = jnp.dot(ya_ref[0], wa_ref[...], preferred_element_type=F32)
    pb = jnp.dot(yb_ref[0], wb_ref[...], preferred_element_type=F32)
    merged = (jax.nn.sigmoid(ma_ref[0].astype(F32)) * pa
              + jax.nn.sigmoid(mb_ref[0].astype(F32)) * pb)
    upd = jnp.dot(merged.astype(BF16), wo_ref[...], preferred_element_type=F32)
    o_ref[0] = x_ref[0] + ada_ref[0, 2:3, :] * upd


def _out_proj(x, ada3, ya, yb, ma, mb, wa, wb, wo, tm):
    bsz, seq, d = x.shape
    tok = lambda width: pl.BlockSpec((1, tm, width), lambda b, i: (b, i, 0))
    return pl.pallas_call(
        _out_kernel,
        grid=(bsz, seq // tm),
        in_specs=[tok(d), pl.BlockSpec((1, 3, d), lambda b, i: (b, 0, 0)),
                  tok(WIDTH), tok(WIDTH), tok(d), tok(d),
                  _const_spec(wa.shape), _const_spec(wb.shape), _const_spec(wo.shape)],
        out_specs=tok(d),
        out_shape=jax.ShapeDtypeStruct((bsz, seq, d), x.dtype),
        compiler_params=pltpu.CompilerParams(
            dimension_semantics=("parallel", "arbitrary"),
            vmem_limit_bytes=VMEM_LIMIT),
        name="out_proj",
    )(x, ada3, ya, yb, ma, mb, wa, wb, wo)


def _tile_heads(vec, pad_to):
    v = jnp.pad(vec.astype(F32), (0, pad_to - vec.shape[0]))
    return jnp.tile(v, N_HEADS).reshape(1, N_HEADS * pad_to)


def _block_avg(sizes, total):
    idx = jnp.arange(total)
    gid = jnp.full((total,), -1, jnp.int32)
    inv = jnp.zeros((total,), F32)
    start = 0
    for n, (size, live) in enumerate(sizes):
        inside = (idx >= start) & (idx < start + size)
        if live:
            gid = jnp.where(inside, n, gid)
            inv = jnp.where(inside, 1.0 / size, inv)
        start += size
    same = (gid[:, None] == gid[None, :]) & (gid[:, None] >= 0)
    return jnp.where(same, inv[None, :], 0.0).astype(BF16)


def kernel(x, c, positions, w_ada, b_ada, norm_w, w_in, b_f, q_lora_norm_w,
           kv_lora_norm_w, w_uq, w_ukv, qn_nope_a, qn_rope_a, kn_nope_a,
           kn_rope_a, qn_b, kn_b, w_branch_a, w_branch_b, w_out):
    bsz, seq, d = x.shape
    depth = w_in.shape[0]
    tm = 512
    tq = 512
    pad = HEAD_PAD - NOPE_DIM - ROPE_DIM

    inv_freq = ROPE_THETA ** (-jnp.arange(0, ROPE_DIM, 2, dtype=F32) / ROPE_DIM)
    invf = jnp.concatenate([jnp.zeros((ROPE_LANE0,), F32), inv_freq, inv_freq,
                            jnp.zeros((pad,), F32)]).reshape(1, LANES)
    pos3 = positions.reshape(bsz, seq, 1)
    head = 2 * [(NOPE_DIM, True), (ROPE_DIM, True), (pad, False)]
    g_a = _block_avg(head, MXU_DIM)
    g_b = _block_avg(4 * [(V_DIM, True)], MXU_DIM)
    sm_a = 1.0 / math.sqrt(NOPE_DIM + ROPE_DIM)
    sm_b = 1.0 / math.sqrt(V_DIM)

    for l in range(depth):
        ada3 = _ada_proj(c, w_ada[l], b_ada[l]).reshape(bsz, 3, d)

        wi = w_in[l]
        o = [0]
        for s in (Q_LORA, KV_LORA, ROPE_DIM, WIDTH, WIDTH, WIDTH, WIDTH, N_HEADS,
                  WIDTH, D_MODEL, D_MODEL):
            o.append(o[-1] + s)
        (w_cq, w_ckv, w_kr, w_ga, w_qb, w_kb, w_vb, w_f, w_gb, w_ma,
         w_mb) = [wi[:, o[n]:o[n + 1]] for n in range(11)]
        w_krf = jnp.concatenate(
            [w_f, jnp.zeros((d, ROPE_LANE0 - N_HEADS), F32), w_kr,
             jnp.zeros((d, pad), F32)], axis=1)
        win = jnp.concatenate([w_cq, w_ckv, w_krf, w_qb, w_kb, w_vb, w_ga, w_gb,
                               w_ma, w_mb], axis=1).astype(BF16)

        wuq = jnp.pad(w_uq[l].reshape(Q_LORA, N_HEADS, NOPE_DIM + ROPE_DIM),
                      ((0, 0), (0, 0), (0, pad))
                      ).reshape(Q_LORA, N_HEADS * HEAD_PAD).astype(BF16)
        wkv = w_ukv[l].reshape(KV_LORA, N_HEADS, NOPE_DIM + V_DIM)
        wuk = jnp.pad(wkv[:, :, :NOPE_DIM], ((0, 0), (0, 0), (0, HEAD_PAD - NOPE_DIM))
                      ).reshape(KV_LORA, N_HEADS * HEAD_PAD).astype(BF16)
        wuv = wkv[:, :, NOPE_DIM:].reshape(KV_LORA, WIDTH).astype(BF16)

        qaw = _tile_heads(jnp.concatenate([qn_nope_a[l], qn_rope_a[l]]) * sm_a, HEAD_PAD)
        kaw = _tile_heads(kn_nope_a[l], HEAD_PAD)
        krw = jnp.concatenate([jnp.zeros((ROPE_LANE0,), F32), kn_rope_a[l],
                               jnp.zeros((pad,), F32)]).reshape(1, LANES)
        qbw = _tile_heads(qn_b[l] * sm_b, V_DIM)
        kbw = _tile_heads(kn_b[l], V_DIM)

        consts = [norm_w[l].reshape(1, d), win, wuq, wuk, wuv, g_a, g_b,
                  q_lora_norm_w[l].reshape(1, Q_LORA),
                  kv_lora_norm_w[l].reshape(1, KV_LORA), qaw, kaw, krw, qbw, kbw, invf]
        (qa, ka, va, qb, kb, vb, f, ga, gb, ma, mb) = _in_proj(x, ada3, pos3, consts, tm)

        f_t = jnp.swapaxes(f, 1, 2).reshape(bsz * N_HEADS, seq)
        b_rows = jnp.tile(b_f[l].astype(F32), bsz).reshape(bsz * N_HEADS, 1)
        cum = _fox_cumsum(f_t, b_rows).reshape(bsz, N_HEADS // 2, 2, seq)

        ya = _attention(qa, ka, va, ga, None, tq=tq, head_lanes=HEAD_PAD, chunk=CHUNK)
        yb = _attention(qb, kb, vb, gb, cum, tq=tq, head_lanes=V_DIM, chunk=1)

        x = _out_proj(x, ada3, ya, yb, ma, mb, w_branch_a[l].astype(BF16),
                      w_branch_b[l].astype(BF16), w_out[l].astype(BF16), tm)
    return x
```

```python
import functools
import math

import jax
import jax.numpy as jnp
from jax import lax
from jax.experimental import pallas as pl
from jax.experimental.pallas import tpu as pltpu

D_MODEL = 1024
CHUNK = 64
N_HEADS = 8
NOPE_DIM = 64
ROPE_DIM = 32
V_DIM = 64
Q_LORA = 384
KV_LORA = 256
WIDTH = N_HEADS * V_DIM
ROPE_THETA = 10000.0
EPS = 1e-6
LOG2E = math.log2(math.e)

LANES = 128
HEAD_PAD = 128
QK_WIDTH = N_HEADS * HEAD_PAD
MXU_DIM = 256
BF16_ROWS = 16
VMEM_LIMIT = 48 * 1024 * 1024

C_CQ = 0
C_CKV = C_CQ + Q_LORA
C_KR = C_CKV + KV_LORA
C_QB = C_KR + LANES
C_KB = C_QB + WIDTH
C_VB = C_KB + WIDTH
C_GA = C_VB + WIDTH
C_GB = C_GA + WIDTH
C_MA = C_GB + WIDTH
C_MB = C_MA + D_MODEL
C_END = C_MB + D_MODEL
ROPE_LANE0 = NOPE_DIM
BIAS_LANE0 = V_DIM
BIAS_PARTS = 3

BF16 = jnp.bfloat16
F32 = jnp.float32


def _const_spec(shape):
    nd = len(shape)
    return pl.BlockSpec(shape, lambda *_: (0,) * nd, pipeline_mode=pl.Buffered(1))


def _ada_kernel(c_ref, w_ref, b_ref, o_ref):
    o_ref[...] = jnp.dot(c_ref[...].astype(BF16), w_ref[...].astype(BF16),
                         preferred_element_type=F32) + b_ref[...]


def _ada_proj(c, w_ada, b_ada):
    bsz, d = c.shape
    n = w_ada.shape[1]
    tn = 512
    return pl.pallas_call(
        _ada_kernel,
        grid=(n // tn,),
        in_specs=[pl.BlockSpec((bsz, d), lambda j: (0, 0)),
                  pl.BlockSpec((d, tn), lambda j: (0, j)),
                  pl.BlockSpec((1, tn), lambda j: (0, j))],
        out_specs=pl.BlockSpec((bsz, tn), lambda j: (0, j)),
        out_shape=jax.ShapeDtypeStruct((bsz, n), F32),
        name="ada_proj",
    )(c, w_ada, b_ada.reshape(1, n))


def _rms(x, n):
    return lax.rsqrt(jnp.sum(x * x, axis=-1, keepdims=True) * (1.0 / n) + EPS)


def _group_mean_sq(x, g_ref):
    g = g_ref[...]
    sq = (x * x).astype(BF16)
    parts = [jnp.dot(sq[:, c:c + MXU_DIM], g, preferred_element_type=F32)
             for c in range(0, x.shape[1], MXU_DIM)]
    return jnp.concatenate(parts, axis=1)


def _rot_half(x):
    lane = lax.broadcasted_iota(jnp.int32, x.shape, 1)
    half = ROPE_DIM // 2
    return jnp.where(lane < ROPE_LANE0 + half,
                     pltpu.roll(x, LANES - half, axis=1),
                     pltpu.roll(x, half, axis=1))


def _head_tiles(x):
    tiles = []
    for hd in range(N_HEADS):
        blk = x[:, (hd // 2) * LANES:(hd // 2 + 1) * LANES]
        tiles.append(pltpu.roll(blk, V_DIM, axis=1) if hd % 2 else blk)
    return tiles


def _in_proj_kernel(x_ref, ada_ref, pos_ref, nw_ref, win_ref, wuq_ref, wuk_ref,
                    wuv_ref, ga_mat_ref, gb_mat_ref, qlw_ref, kvlw_ref, qaw_ref,
                    kaw_ref, krw_ref, qbw_ref, kbw_ref, invf_ref, bf_ref, place_ref,
                    qa_ref, ka_ref, vat_ref, qb_ref, kb_ref, vbt_ref,
                    ga_ref, gb_ref, ma_ref, mb_ref, carry_ref):
    tm = x_ref.shape[1]
    x = x_ref[0]
    shift = ada_ref[0, 0:1, :]
    scale = ada_ref[0, 1:2, :]
    h = (x * _rms(x, D_MODEL)) * nw_ref[...] * (1.0 + scale) + shift
    hb = h.astype(BF16)

    def proj(lo, hi):
        return jnp.dot(hb, win_ref[:, lo:hi], preferred_element_type=F32)

    ang = pos_ref[0].astype(F32) * invf_ref[...]
    lane = lax.broadcasted_iota(jnp.int32, ang.shape, 1)
    cos_t = jnp.cos(ang)
    sin_t = jnp.sin(ang)
    sin_t = jnp.where(lane < ROPE_LANE0 + ROPE_DIM // 2, -sin_t, sin_t)

    def rope(t):
        return t * cos_t + _rot_half(t) * sin_t

    lat = proj(C_CQ, C_QB)
    cq = lat[:, C_CQ:C_CKV]
    ckv = lat[:, C_CKV:C_KR]
    kr = lat[:, C_KR:C_QB]

    cqn = (cq * _rms(cq, Q_LORA) * qlw_ref[...]).astype(BF16)
    q = jnp.dot(cqn, wuq_ref[...], preferred_element_type=F32)
    qn = q * lax.rsqrt(_group_mean_sq(q, ga_mat_ref) + EPS) * qaw_ref[...]
    for hd in range(N_HEADS):
        sl = slice(hd * HEAD_PAD, (hd + 1) * HEAD_PAD)
        qa_ref[0, :, sl] = rope(qn[:, sl]).astype(BF16)

    ckvn = (ckv * _rms(ckv, KV_LORA) * kvlw_ref[...]).astype(BF16)
    k = jnp.dot(ckvn, wuk_ref[...], preferred_element_type=F32)
    kn = k * lax.rsqrt(_group_mean_sq(k, ga_mat_ref) + EPS) * kaw_ref[...]
    rope_lane = (lane >= ROPE_LANE0) & (lane < ROPE_LANE0 + ROPE_DIM)
    kr_ms = jnp.sum(jnp.where(rope_lane, kr * kr, 0.0), axis=-1, keepdims=True)
    krn = kr * lax.rsqrt(kr_ms * (1.0 / ROPE_DIM) + EPS) * krw_ref[...]
    krr = rope(krn)
    for hd in range(N_HEADS):
        sl = slice(hd * HEAD_PAD, (hd + 1) * HEAD_PAD)
        ka_ref[0, :, sl] = (kn[:, sl] + krr).astype(BF16)
    va = jnp.dot(ckvn, wuv_ref[...], preferred_element_type=F32)
    vat_ref[0] = va.T.astype(BF16)

    z = kr + bf_ref[...]
    logf = jnp.minimum(z, 0.0) - jnp.log1p(jnp.exp(-jnp.abs(z)))
    c = jnp.where(lane < N_HEADS, logf, 0.0)
    row = lax.broadcasted_iota(jnp.int32, c.shape, 0)
    step = 1
    while step < tm:
        c = c + jnp.where(row >= step, pltpu.roll(c, step, axis=0), 0.0)
        step *= 2

    @pl.when(pl.program_id(1) == 0)
    def _():
        carry_ref[...] = jnp.zeros_like(carry_ref)

    c = c + carry_ref[...]
    carry_ref[...] = c[tm - 1:tm, :]
    nb = c * (-LOG2E)
    hi = nb.astype(BF16).astype(F32)
    mid = (nb - hi).astype(BF16).astype(F32)
    lo = nb - hi - mid
    packed = jnp.where(lane < N_HEADS, hi,
                       jnp.where(lane < 2 * N_HEADS, pltpu.roll(mid, N_HEADS, axis=1),
                                 pltpu.roll(lo, 2 * N_HEADS, axis=1)))
    bias = jnp.dot(packed.astype(BF16), place_ref[...], preferred_element_type=F32)

    head_lane = lane < V_DIM
    ones_t = jnp.where((lane >= BIAS_LANE0) & (lane < BIAS_LANE0 + BIAS_PARTS), 1.0, 0.0)
    qb = proj(C_QB, C_KB)
    qbn = qb * lax.rsqrt(_group_mean_sq(qb, gb_mat_ref) + EPS) * qbw_ref[...]
    for hd, t in enumerate(_head_tiles(qbn)):
        qb_ref[0, :, hd * HEAD_PAD:(hd + 1) * HEAD_PAD] = jnp.where(
            head_lane, t, ones_t).astype(BF16)
    kb = proj(C_KB, C_VB)
    kbn = kb * lax.rsqrt(_group_mean_sq(kb, gb_mat_ref) + EPS) * kbw_ref[...]
    for hd, t in enumerate(_head_tiles(kbn)):
        sl = slice(hd * HEAD_PAD, (hd + 1) * HEAD_PAD)
        kb_ref[0, :, sl] = jnp.where(head_lane, t, bias[:, sl]).astype(BF16)
    vbt_ref[0] = proj(C_VB, C_GA).T.astype(BF16)

    ga_ref[0] = proj(C_GA, C_GB).astype(BF16)
    gb_ref[0] = proj(C_GB, C_MA).astype(BF16)
    ma_ref[0] = proj(C_MA, C_MB).astype(BF16)
    mb_ref[0] = proj(C_MB, C_END).astype(BF16)


def _in_proj(x, ada3, pos3, consts, tm):
    bsz, seq, d = x.shape
    grid = (bsz, seq // tm)
    tok = lambda width: pl.BlockSpec((1, tm, width), lambda b, i: (b, i, 0))
    tok_t = pl.BlockSpec((1, WIDTH, tm), lambda b, i: (b, 0, i))
    in_specs = [tok(d),
                pl.BlockSpec((1, 3, d), lambda b, i: (b, 0, 0)),
                tok(1)] + [_const_spec(a.shape) for a in consts]
    row_major = lambda w: jax.ShapeDtypeStruct((bsz, seq, w), BF16)
    transposed = jax.ShapeDtypeStruct((bsz, WIDTH, seq), BF16)
    out_shape = [row_major(QK_WIDTH), row_major(QK_WIDTH), transposed,
                 row_major(QK_WIDTH), row_major(QK_WIDTH), transposed,
                 row_major(WIDTH), row_major(WIDTH), row_major(d), row_major(d)]
    out_specs = [tok(QK_WIDTH), tok(QK_WIDTH), tok_t,
                 tok(QK_WIDTH), tok(QK_WIDTH), tok_t,
                 tok(WIDTH), tok(WIDTH), tok(d), tok(d)]
    return pl.pallas_call(
        _in_proj_kernel,
        grid=grid,
        in_specs=in_specs,
        out_specs=out_specs,
        out_shape=out_shape,
        scratch_shapes=[pltpu.VMEM((1, LANES), F32)],
        compiler_params=pltpu.CompilerParams(
            dimension_semantics=("parallel", "arbitrary"),
            vmem_limit_bytes=VMEM_LIMIT),
        name="in_proj",
    )(x, ada3, pos3, *consts)


def _attn_kernel(q_ref, k_ref, vt_ref, g_ref, o_ref, *, tq, chunk, n_q_tiles):
    tk = tq
    nt = (((1,), (1,)), ((), ()))
    key = lax.broadcasted_iota(jnp.int32, (tk, tq), 0)
    qry = lax.broadcasted_iota(jnp.int32, (tk, tq), 1)
    allowed = (key // chunk) <= (qry // chunk)
    ones = jnp.ones((BF16_ROWS, tk), BF16)

    def scores(hh, j, masked):
        sl = slice(hh * HEAD_PAD, (hh + 1) * HEAD_PAD)
        st = lax.dot_general(k_ref[0, j * tk:(j + 1) * tk, sl], q_ref[0, :, sl], nt,
                             preferred_element_type=F32)
        if masked:
            st = jnp.where(allowed, st, -jnp.inf)
        return st, jnp.max(st, axis=0, keepdims=True)

    def update(hh, j, st, mx, state):
        vt = jnp.concatenate(
            [vt_ref[0, hh * V_DIM:(hh + 1) * V_DIM, j * tk:(j + 1) * tk], ones], axis=0)
        if state is None:
            p = jnp.exp2(st - mx).astype(BF16)
            return mx, jnp.dot(vt, p, preferred_element_type=F32)
        m, acc = state
        m_new = jnp.maximum(m, mx)
        p = jnp.exp2(st - m_new).astype(BF16)
        acc = jnp.exp2(m - m_new) * acc + jnp.dot(vt, p, preferred_element_type=F32)
        return m_new, acc

    def run(n_tiles):
        state = [None, None]
        nxt = [scores(hh, 0, n_tiles == 1) for hh in range(2)]
        for j in range(n_tiles):
            cur = nxt
            if j + 1 < n_tiles:
                nxt = [scores(hh, j + 1, j + 2 == n_tiles) for hh in range(2)]
            state = [update(hh, j, cur[hh][0], cur[hh][1], state[hh]) for hh in range(2)]
        halves = [acc[:V_DIM] / acc[V_DIM:V_DIM + 1] for _, acc in state]
        o = jnp.concatenate(halves, axis=0).T
        g = g_ref[0].astype(F32)
        o_ref[0] = (o * (g * jax.nn.sigmoid(g))).astype(BF16)

    qi = pl.program_id(2)
    for t in range(n_q_tiles):
        pl.when(qi == t)(functools.partial(run, t + 1))


def _attention(q, k, vt, gate, *, tq, chunk, name):
    bsz, seq, _ = q.shape
    pairs = N_HEADS // 2
    n_q_tiles = seq // tq
    kern = functools.partial(_attn_kernel, tq=tq, chunk=chunk, n_q_tiles=n_q_tiles)
    return pl.pallas_call(
        kern,
        grid=(bsz, pairs, n_q_tiles),
        in_specs=[pl.BlockSpec((1, tq, 2 * HEAD_PAD), lambda b, p, i: (b, i, p)),
                  pl.BlockSpec((1, seq, 2 * HEAD_PAD), lambda b, p, i: (b, 0, p)),
                  pl.BlockSpec((1, 2 * V_DIM, seq), lambda b, p, i: (b, p, 0)),
                  pl.BlockSpec((1, tq, 2 * V_DIM), lambda b, p, i: (b, i, p))],
        out_specs=pl.BlockSpec((1, tq, 2 * V_DIM), lambda b, p, i: (b, i, p)),
        out_shape=jax.ShapeDtypeStruct((bsz, seq, WIDTH), BF16),
        compiler_params=pltpu.CompilerParams(
            dimension_semantics=("parallel", "parallel", "arbitrary"),
            vmem_limit_bytes=VMEM_LIMIT),
        name=name,
    )(q, k, vt, gate)


def _out_kernel(x_ref, ada_ref, ya_ref, yb_ref, ma_ref, mb_ref, wa_ref, wb_ref,
                wo_ref, o_ref):
    pa = jnp.dot(ya_ref[0], wa_ref[...], preferred_element_type=F32)
    pb = jnp.dot(yb_ref[0], wb_ref[...], preferred_element_type=F32)
    merged = (jax.nn.sigmoid(ma_ref[0].astype(F32)) * pa
              + jax.nn.sigmoid(mb_ref[0].astype(F32)) * pb)
    upd = jnp.dot(merged.astype(BF16), wo_ref[...], preferred_element_type=F32)
    o_ref[0] = x_ref[0] + ada_ref[0, 2:3, :] * upd


def _out_proj(x, ada3, ya, yb, ma, mb, wa, wb, wo, tm):
    bsz, seq, d = x.shape
    tok = lambda width: pl.BlockSpec((1, tm, width), lambda b, i: (b, i, 0))
    return pl.pallas_call(
        _out_kernel,
        grid=(bsz, seq // tm),
        in_specs=[tok(d), pl.BlockSpec((1, 3, d), lambda b, i: (b, 0, 0)),
                  tok(WIDTH), tok(WIDTH), tok(d), tok(d),
                  _const_spec(wa.shape), _const_spec(wb.shape), _const_spec(wo.shape)],
        out_specs=tok(d),
        out_shape=jax.ShapeDtypeStruct((bsz, seq, d), x.dtype),
        compiler_params=pltpu.CompilerParams(
            dimension_semantics=("parallel", "arbitrary"),
            vmem_limit_bytes=VMEM_LIMIT),
        name="out_proj",
    )(x, ada3, ya, yb, ma, mb, wa, wb, wo)


def _tile_heads(vec, pad_to):
    v = jnp.pad(vec.astype(F32), (0, pad_to - vec.shape[0]))
    return jnp.tile(v, N_HEADS).reshape(1, N_HEADS * pad_to)


def _block_avg(sizes, total):
    idx = jnp.arange(total)
    gid = jnp.full((total,), -1, jnp.int32)
    inv = jnp.zeros((total,), F32)
    start = 0
    for n, (size, live) in enumerate(sizes):
        inside = (idx >= start) & (idx < start + size)
        if live:
            gid = jnp.where(inside, n, gid)
            inv = jnp.where(inside, 1.0 / size, inv)
        start += size
    same = (gid[:, None] == gid[None, :]) & (gid[:, None] >= 0)
    return jnp.where(same, inv[None, :], 0.0).astype(BF16)


def _bias_placement():
    src = jnp.arange(LANES)[:, None]
    dst = jnp.arange(QK_WIDTH)[None, :]
    part, head = src // N_HEADS, src % N_HEADS
    hit = (part < BIAS_PARTS) & (dst == head * HEAD_PAD + BIAS_LANE0 + part)
    return hit.astype(BF16)


def kernel(x, c, positions, w_ada, b_ada, norm_w, w_in, b_f, q_lora_norm_w,
           kv_lora_norm_w, w_uq, w_ukv, qn_nope_a, qn_rope_a, kn_nope_a,
           kn_rope_a, qn_b, kn_b, w_branch_a, w_branch_b, w_out):
    bsz, seq, d = x.shape
    depth = w_in.shape[0]
    tm = 512
    tq = 512
    pad = HEAD_PAD - NOPE_DIM - ROPE_DIM

    inv_freq = ROPE_THETA ** (-jnp.arange(0, ROPE_DIM, 2, dtype=F32) / ROPE_DIM)
    invf = jnp.concatenate([jnp.zeros((ROPE_LANE0,), F32), inv_freq, inv_freq,
                            jnp.zeros((pad,), F32)]).reshape(1, LANES)
    pos3 = positions.reshape(bsz, seq, 1)
    head = 2 * [(NOPE_DIM, True), (ROPE_DIM, True), (pad, False)]
    g_a = _block_avg(head, MXU_DIM)
    g_b = _block_avg(4 * [(V_DIM, True)], MXU_DIM)
    place = _bias_placement()
    sm_a = LOG2E / math.sqrt(NOPE_DIM + ROPE_DIM)
    sm_b = LOG2E / math.sqrt(V_DIM)

    for l in range(depth):
        ada3 = _ada_proj(c, w_ada[l], b_ada[l]).reshape(bsz, 3, d)

        wi = w_in[l]
        o = [0]
        for s in (Q_LORA, KV_LORA, ROPE_DIM, WIDTH, WIDTH, WIDTH, WIDTH, N_HEADS,
                  WIDTH, D_MODEL, D_MODEL):
            o.append(o[-1] + s)
        (w_cq, w_ckv, w_kr, w_ga, w_qb, w_kb, w_vb, w_f, w_gb, w_ma,
         w_mb) = [wi[:, o[n]:o[n + 1]] for n in range(11)]
        w_krf = jnp.concatenate(
            [w_f, jnp.zeros((d, ROPE_LANE0 - N_HEADS), F32), w_kr,
             jnp.zeros((d, pad), F32)], axis=1)
        win = jnp.concatenate([w_cq, w_ckv, w_krf, w_qb, w_kb, w_vb, w_ga, w_gb,
                               w_ma, w_mb], axis=1).astype(BF16)

        wuq = jnp.pad(w_uq[l].reshape(Q_LORA, N_HEADS, NOPE_DIM + ROPE_DIM),
                      ((0, 0), (0, 0), (0, pad))
                      ).reshape(Q_LORA, QK_WIDTH).astype(BF16)
        wkv = w_ukv[l].reshape(KV_LORA, N_HEADS, NOPE_DIM + V_DIM)
        wuk = jnp.pad(wkv[:, :, :NOPE_DIM], ((0, 0), (0, 0), (0, HEAD_PAD - NOPE_DIM))
                      ).reshape(KV_LORA, QK_WIDTH).astype(BF16)
        wuv = wkv[:, :, NOPE_DIM:].reshape(KV_LORA, WIDTH).astype(BF16)

        qaw = _tile_heads(jnp.concatenate([qn_nope_a[l], qn_rope_a[l]]) * sm_a, HEAD_PAD)
        kaw = _tile_heads(kn_nope_a[l], HEAD_PAD)
        krw = jnp.concatenate([jnp.zeros((ROPE_LANE0,), F32), kn_rope_a[l],
                               jnp.zeros((pad,), F32)]).reshape(1, LANES)
        qbw = _tile_heads(qn_b[l] * sm_b, V_DIM)
        kbw = _tile_heads(kn_b[l], V_DIM)
        bfv = jnp.pad(b_f[l].astype(F32), (0, LANES - N_HEADS)).reshape(1, LANES)

        consts = [norm_w[l].reshape(1, d), win, wuq, wuk, wuv, g_a, g_b,
                  q_lora_norm_w[l].reshape(1, Q_LORA),
                  kv_lora_norm_w[l].reshape(1, KV_LORA), qaw, kaw, krw, qbw, kbw,
                  invf, bfv, place]
        (qa, ka, vat, qb, kb, vbt, ga, gb, ma, mb) = _in_proj(x, ada3, pos3, consts, tm)

        ya = _attention(qa, ka, vat, ga, tq=tq, chunk=CHUNK, name="mla_attn")
        yb = _attention(qb, kb, vbt, gb, tq=tq, chunk=1, name="fox_attn")

        x = _out_proj(x, ada3, ya, yb, ma, mb, w_branch_a[l].astype(BF16),
                      w_branch_b[l].astype(BF16), w_out[l].astype(BF16), tm)
    return x
```

```python
import functools
import math

import jax
import jax.numpy as jnp
from jax import lax
from jax.experimental import pallas as pl
from jax.experimental.pallas import tpu as pltpu

D_MODEL = 1024
CHUNK = 64
N_HEADS = 8
NOPE_DIM = 64
ROPE_DIM = 32
V_DIM = 64
Q_LORA = 384
KV_LORA = 256
WIDTH = N_HEADS * V_DIM
ROPE_THETA = 10000.0
EPS = 1e-6
LOG2E = math.log2(math.e)

LANES = 128
HEAD_PAD = 128
QK_WIDTH = N_HEADS * HEAD_PAD
MXU_DIM = 256
BF16_ROWS = 16
VMEM_LIMIT = 48 * 1024 * 1024

C_CQ = 0
C_CKV = C_CQ + Q_LORA
C_KR = C_CKV + KV_LORA
C_QB = C_KR + LANES
C_KB = C_QB + WIDTH
C_VB = C_KB + WIDTH
C_GA = C_VB + WIDTH
C_GB = C_GA + WIDTH
C_MA = C_GB + WIDTH
C_MB = C_MA + D_MODEL
C_END = C_MB + D_MODEL
ROPE_LANE0 = NOPE_DIM
BIAS_LANE0 = V_DIM
BIAS_PARTS = 3

BF16 = jnp.bfloat16
F32 = jnp.float32


def _const_spec(shape):
    nd = len(shape)
    return pl.BlockSpec(shape, lambda *_: (0,) * nd, pipeline_mode=pl.Buffered(1))


def _ada_kernel(c_ref, w_ref, b_ref, o_ref):
    o_ref[...] = jnp.dot(c_ref[...].astype(BF16), w_ref[...].astype(BF16),
                         preferred_element_type=F32) + b_ref[...]


def _ada_proj(c, w_ada, b_ada):
    bsz, d = c.shape
    n = w_ada.shape[1]
    tn = 512
    return pl.pallas_call(
        _ada_kernel,
        grid=(n // tn,),
        in_specs=[pl.BlockSpec((bsz, d), lambda j: (0, 0)),
                  pl.BlockSpec((d, tn), lambda j: (0, j)),
                  pl.BlockSpec((1, tn), lambda j: (0, j))],
        out_specs=pl.BlockSpec((bsz, tn), lambda j: (0, j)),
        out_shape=jax.ShapeDtypeStruct((bsz, n), F32),
        name="ada_proj",
    )(c, w_ada, b_ada.reshape(1, n))


def _rms(x, n):
    return lax.rsqrt(jnp.sum(x * x, axis=-1, keepdims=True) * (1.0 / n) + EPS)


def _group_mean_sq(x, g_ref):
    g = g_ref[...]
    sq = (x * x).astype(BF16)
    parts = [jnp.dot(sq[:, c:c + MXU_DIM], g, preferred_element_type=F32)
             for c in range(0, x.shape[1], MXU_DIM)]
    return jnp.concatenate(parts, axis=1)


def _rot_half(x):
    lane = lax.broadcasted_iota(jnp.int32, x.shape, 1)
    half = ROPE_DIM // 2
    return jnp.where(lane < ROPE_LANE0 + half,
                     pltpu.roll(x, LANES - half, axis=1),
                     pltpu.roll(x, half, axis=1))


def _head_tiles(x):
    tiles = []
    for hd in range(N_HEADS):
        blk = x[:, (hd // 2) * LANES:(hd // 2 + 1) * LANES]
        tiles.append(pltpu.roll(blk, V_DIM, axis=1) if hd % 2 else blk)
    return tiles


def _in_proj_kernel(x_ref, ada_ref, pos_ref, nw_ref, win_ref, wuq_ref, wuk_ref,
                    wuv_ref, ga_mat_ref, gb_mat_ref, qlw_ref, kvlw_ref, qaw_ref,
                    kaw_ref, krw_ref, qbw_ref, kbw_ref, invf_ref, bf_ref, place_ref,
                    qa_ref, ka_ref, vat_ref, qb_ref, kb_ref, vbt_ref,
                    ga_ref, gb_ref, ma_ref, mb_ref, carry_ref):
    tm = x_ref.shape[1]
    x = x_ref[0]
    shift = ada_ref[0, 0:1, :]
    scale = ada_ref[0, 1:2, :]
    h = (x * _rms(x, D_MODEL)) * nw_ref[...] * (1.0 + scale) + shift
    hb = h.astype(BF16)

    def proj(lo, hi):
        return jnp.dot(hb, win_ref[:, lo:hi], preferred_element_type=F32)

    ang = pos_ref[0].astype(F32) * invf_ref[...]
    lane = lax.broadcasted_iota(jnp.int32, ang.shape, 1)
    cos_t = jnp.cos(ang)
    sin_t = jnp.sin(ang)
    sin_t = jnp.where(lane < ROPE_LANE0 + ROPE_DIM // 2, -sin_t, sin_t)

    def rope(t):
        return t * cos_t + _rot_half(t) * sin_t

    lat = proj(C_CQ, C_QB)
    cq = lat[:, C_CQ:C_CKV]
    ckv = lat[:, C_CKV:C_KR]
    kr = lat[:, C_KR:C_QB]

    cqn = (cq * _rms(cq, Q_LORA) * qlw_ref[...]).astype(BF16)
    q = jnp.dot(cqn, wuq_ref[...], preferred_element_type=F32)
    qn = q * lax.rsqrt(_group_mean_sq(q, ga_mat_ref) + EPS) * qaw_ref[...]
    for hd in range(N_HEADS):
        sl = slice(hd * HEAD_PAD, (hd + 1) * HEAD_PAD)
        qa_ref[0, :, sl] = rope(qn[:, sl]).astype(BF16)

    ckvn = (ckv * _rms(ckv, KV_LORA) * kvlw_ref[...]).astype(BF16)
    k = jnp.dot(ckvn, wuk_ref[...], preferred_element_type=F32)
    kn = k * lax.rsqrt(_group_mean_sq(k, ga_mat_ref) + EPS) * kaw_ref[...]
    rope_lane = (lane >= ROPE_LANE0) & (lane < ROPE_LANE0 + ROPE_DIM)
    kr_ms = jnp.sum(jnp.where(rope_lane, kr * kr, 0.0), axis=-1, keepdims=True)
    krn = kr * lax.rsqrt(kr_ms * (1.0 / ROPE_DIM) + EPS) * krw_ref[...]
    krr = rope(krn)
    for hd in range(N_HEADS):
        sl = slice(hd * HEAD_PAD, (hd + 1) * HEAD_PAD)
        ka_ref[0, :, sl] = (kn[:, sl] + krr).astype(BF16)
    va = jnp.dot(ckvn, wuv_ref[...], preferred_element_type=F32)
    vat_ref[0] = va.T.astype(BF16)

    z = kr + bf_ref[...]
    logf = jnp.minimum(z, 0.0) - jnp.log1p(jnp.exp(-jnp.abs(z)))
    c = jnp.where(lane < N_HEADS, logf, 0.0)
    row = lax.broadcasted_iota(jnp.int32, c.shape, 0)
    step = 1
    while step < tm:
        c = c + jnp.where(row >= step, pltpu.roll(c, step, axis=0), 0.0)
        step *= 2

    @pl.when(pl.program_id(1) == 0)
    def _():
        carry_ref[...] = jnp.zeros_like(carry_ref)

    c = c + carry_ref[...]
    carry_ref[...] = c[tm - 1:tm, :]
    nb = c * (-LOG2E)
    hi = nb.astype(BF16).astype(F32)
    mid = (nb - hi).astype(BF16).astype(F32)
    lo = nb - hi - mid
    packed = jnp.where(lane < N_HEADS, hi,
                       jnp.where(lane < 2 * N_HEADS, pltpu.roll(mid, N_HEADS, axis=1),
                                 pltpu.roll(lo, 2 * N_HEADS, axis=1)))
    bias = jnp.dot(packed.astype(BF16), place_ref[...], preferred_element_type=F32)

    head_lane = lane < V_DIM
    ones_t = jnp.where((lane >= BIAS_LANE0) & (lane < BIAS_LANE0 + BIAS_PARTS), 1.0, 0.0)
    qb = proj(C_QB, C_KB)
    qbn = qb * lax.rsqrt(_group_mean_sq(qb, gb_mat_ref) + EPS) * qbw_ref[...]
    for hd, t in enumerate(_head_tiles(qbn)):
        qb_ref[0, :, hd * HEAD_PAD:(hd + 1) * HEAD_PAD] = jnp.where(
            head_lane, t, ones_t).astype(BF16)
    kb = proj(C_KB, C_VB)
    kbn = kb * lax.rsqrt(_group_mean_sq(kb, gb_mat_ref) + EPS) * kbw_ref[...]
    for hd, t in enumerate(_head_tiles(kbn)):
        sl = slice(hd * HEAD_PAD, (hd + 1) * HEAD_PAD)
        kb_ref[0, :, sl] = jnp.where(head_lane, t, bias[:, sl]).astype(BF16)
    vbt_ref[0] = proj(C_VB, C_GA).T.astype(BF16)

    ga_ref[0] = proj(C_GA, C_GB).astype(BF16)
    gb_ref[0] = proj(C_GB, C_MA).astype(BF16)
    ma_ref[0] = proj(C_MA, C_MB).astype(BF16)
    mb_ref[0] = proj(C_MB, C_END).astype(BF16)


def _in_proj(x, ada3, pos3, consts, tm):
    bsz, seq, d = x.shape
    grid = (bsz, seq // tm)
    tok = lambda width: pl.BlockSpec((1, tm, width), lambda b, i: (b, i, 0))
    tok_t = pl.BlockSpec((1, WIDTH, tm), lambda b, i: (b, 0, i))
    in_specs = [tok(d),
                pl.BlockSpec((1, 3, d), lambda b, i: (b, 0, 0)),
                tok(1)] + [_const_spec(a.shape) for a in consts]
    row_major = lambda w: jax.ShapeDtypeStruct((bsz, seq, w), BF16)
    transposed = jax.ShapeDtypeStruct((bsz, WIDTH, seq), BF16)
    out_shape = [row_major(QK_WIDTH), row_major(QK_WIDTH), transposed,
                 row_major(QK_WIDTH), row_major(QK_WIDTH), transposed,
                 row_major(WIDTH), row_major(WIDTH), row_major(d), row_major(d)]
    out_specs = [tok(QK_WIDTH), tok(QK_WIDTH), tok_t,
                 tok(QK_WIDTH), tok(QK_WIDTH), tok_t,
                 tok(WIDTH), tok(WIDTH), tok(d), tok(d)]
    return pl.pallas_call(
        _in_proj_kernel,
        grid=grid,
        in_specs=in_specs,
        out_specs=out_specs,
        out_shape=out_shape,
        scratch_shapes=[pltpu.VMEM((1, LANES), F32)],
        compiler_params=pltpu.CompilerParams(
            dimension_semantics=("parallel", "arbitrary"),
            vmem_limit_bytes=VMEM_LIMIT),
        name="in_proj",
    )(x, ada3, pos3, *consts)


def _attn_kernel(qa_ref, ka_ref, vat_ref, ga_ref, qb_ref, kb_ref, vbt_ref, gb_ref,
                 oa_ref, ob_ref, *, tq, n_q_tiles):
    half = tq // 2
    nt = (((1,), (1,)), ((), ()))
    key = lax.broadcasted_iota(jnp.int32, (half, half), 0)
    qry = lax.broadcasted_iota(jnp.int32, (half, half), 1)
    ones = jnp.ones((BF16_ROWS, tq), BF16)
    branches = [(qa_ref, ka_ref, vat_ref, ga_ref, oa_ref, (key // CHUNK) <= (qry // CHUNK)),
                (qb_ref, kb_ref, vbt_ref, gb_ref, ob_ref, key <= qry)]
    chains = [(br, hh) for br in range(2) for hh in range(2)]

    def lanes(hh):
        return slice(hh * HEAD_PAD, (hh + 1) * HEAD_PAD)

    def values(br, hh, lo, hi):
        vt = branches[br][2][0, hh * V_DIM:(hh + 1) * V_DIM, lo:hi]
        return jnp.concatenate([vt, ones[:, :hi - lo]], axis=0)

    def probs(st, m):
        return jnp.exp2((st - m).astype(BF16))

    def full_scores(br, hh, t, j):
        q_ref, k_ref = branches[br][:2]
        st = lax.dot_general(k_ref[0, j * tq:(j + 1) * tq, lanes(hh)],
                             q_ref[0, t * tq:(t + 1) * tq, lanes(hh)], nt,
                             preferred_element_type=F32)
        return st, jnp.max(st, axis=0, keepdims=True)

    def full_pv(br, hh, j, st, m):
        return jnp.dot(values(br, hh, j * tq, (j + 1) * tq), probs(st, m),
                       preferred_element_type=F32)

    def diag_scores(br, hh, t):
        q_ref, k_ref = branches[br][:2]
        mask = branches[br][5]
        base = t * tq
        st_lo = lax.dot_general(k_ref[0, base:base + half, lanes(hh)],
                                q_ref[0, base:base + tq, lanes(hh)], nt,
                                preferred_element_type=F32)
        st_hi = lax.dot_general(k_ref[0, base + half:base + tq, lanes(hh)],
                                q_ref[0, base + half:base + tq, lanes(hh)], nt,
                                preferred_element_type=F32)
        st_lo = jnp.concatenate(
            [jnp.where(mask, st_lo[:, :half], -jnp.inf), st_lo[:, half:]], axis=1)
        st_hi = jnp.where(mask, st_hi, -jnp.inf)
        mx_lo = jnp.max(st_lo, axis=0, keepdims=True)
        mx_hi = jnp.max(st_hi, axis=0, keepdims=True)
        mx = jnp.concatenate(
            [mx_lo[:, :half], jnp.maximum(mx_lo[:, half:], mx_hi)], axis=1)
        return (st_lo, st_hi), mx

    def diag_pv(br, hh, t, st, m):
        st_lo, st_hi = st
        base = t * tq
        pv = jnp.dot(values(br, hh, base, base + half), probs(st_lo, m),
                     preferred_element_type=F32)
        pv_hi = jnp.dot(values(br, hh, base + half, base + tq),
                        probs(st_hi, m[:, half:]), preferred_element_type=F32)
        return jnp.concatenate([pv[:, :half], pv[:, half:] + pv_hi], axis=1)

    def scores(br, hh, t, j):
        return diag_scores(br, hh, t) if j == t else full_scores(br, hh, t, j)

    def update(br, hh, t, j, st, mx, state):
        m_new = mx if state is None else jnp.maximum(state[0], mx)
        pv = diag_pv(br, hh, t, st, m_new) if j == t else full_pv(br, hh, j, st, m_new)
        if state is None:
            return m_new, pv
        m, acc = state
        return m_new, jnp.exp2(m - m_new) * acc + pv

    for t in range(n_q_tiles):
        state = {c: None for c in chains}
        nxt = {c: scores(*c, t, 0) for c in chains}
        for j in range(t + 1):
            cur = nxt
            if j < t:
                nxt = {c: scores(*c, t, j + 1) for c in chains}
            state = {c: update(*c, t, j, *cur[c], state[c]) for c in chains}
        for br in range(2):
            g_ref, o_ref = branches[br][3:5]
            halves = [state[(br, hh)][1] for hh in range(2)]
            halves = [acc[:V_DIM] / acc[V_DIM:V_DIM + 1] for acc in halves]
            o = jnp.concatenate(halves, axis=0).T
            g = g_ref[0, t * tq:(t + 1) * tq, :].astype(F32)
            o_ref[0, t * tq:(t + 1) * tq, :] = (o * (g * jax.nn.sigmoid(g))).astype(BF16)


def _attention(qa, ka, vat, ga, qb, kb, vbt, gb, *, tq):
    bsz, seq, _ = qa.shape
    pairs = N_HEADS // 2
    kern = functools.partial(_attn_kernel, tq=tq, n_q_tiles=seq // tq)
    qk_spec = pl.BlockSpec((1, seq, 2 * HEAD_PAD), lambda b, p: (b, 0, p))
    vt_spec = pl.BlockSpec((1, 2 * V_DIM, seq), lambda b, p: (b, p, 0))
    tok_spec = pl.BlockSpec((1, seq, 2 * V_DIM), lambda b, p: (b, 0, p))
    out = jax.ShapeDtypeStruct((bsz, seq, WIDTH), BF16)
    return pl.pallas_call(
        kern,
        grid=(bsz, pairs),
        in_specs=2 * [qk_spec, qk_spec, vt_spec, tok_spec],
        out_specs=[tok_spec, tok_spec],
        out_shape=[out, out],
        compiler_params=pltpu.CompilerParams(
            dimension_semantics=("parallel", "arbitrary"),
            vmem_limit_bytes=VMEM_LIMIT),
        name="attn",
    )(qa, ka, vat, ga, qb, kb, vbt, gb)


def _out_kernel(x_ref, ada_ref, ya_ref, yb_ref, ma_ref, mb_ref, wa_ref, wb_ref,
                wo_ref, o_ref):
    pa = jnp.dot(ya_ref[0], wa_ref[...], preferred_element_type=F32)
    pb = jnp.dot(yb_ref[0], wb_ref[...], preferred_element_type=F32)
    merged = (jax.nn.sigmoid(ma_ref[0].astype(F32)) * pa
              + jax.nn.sigmoid(mb_ref[0].astype(F32)) * pb)
    upd = jnp.dot(merged.astype(BF16), wo_ref[...], preferred_element_type=F32)
    o_ref[0] = x_ref[0] + ada_ref[0, 2:3, :] * upd


def _out_proj(x, ada3, ya, yb, ma, mb, wa, wb, wo, tm):
    bsz, seq, d = x.shape
    tok = lambda width: pl.BlockSpec((1, tm, width), lambda b, i: (b, i, 0))
    return pl.pallas_call(
        _out_kernel,
        grid=(bsz, seq // tm),
        in_specs=[tok(d), pl.BlockSpec((1, 3, d), lambda b, i: (b, 0, 0)),
                  tok(WIDTH), tok(WIDTH), tok(d), tok(d),
                  _const_spec(wa.shape), _const_spec(wb.shape), _const_spec(wo.shape)],
        out_specs=tok(d),
        out_shape=jax.ShapeDtypeStruct((bsz, seq, d), x.dtype),
        compiler_params=pltpu.CompilerParams(
            dimension_semantics=("parallel", "arbitrary"),
            vmem_limit_bytes=VMEM_LIMIT),
        name="out_proj",
    )(x, ada3, ya, yb, ma, mb, wa, wb, wo)


def _tile_heads(vec, pad_to):
    v = jnp.pad(vec.astype(F32), (0, pad_to - vec.shape[0]))
    return jnp.tile(v, N_HEADS).reshape(1, N_HEADS * pad_to)


def _block_avg(sizes, total):
    idx = jnp.arange(total)
    gid = jnp.full((total,), -1, jnp.int32)
    inv = jnp.zeros((total,), F32)
    start = 0
    for n, (size, live) in enumerate(sizes):
        inside = (idx >= start) & (idx < start + size)
        if live:
            gid = jnp.where(inside, n, gid)
            inv = jnp.where(inside, 1.0 / size, inv)
        start += size
    same = (gid[:, None] == gid[None, :]) & (gid[:, None] >= 0)
    return jnp.where(same, inv[None, :], 0.0).astype(BF16)


def _bias_placement():
    src = jnp.arange(LANES)[:, None]
    dst = jnp.arange(QK_WIDTH)[None, :]
    part, head = src // N_HEADS, src % N_HEADS
    hit = (part < BIAS_PARTS) & (dst == head * HEAD_PAD + BIAS_LANE0 + part)
    return hit.astype(BF16)


def kernel(x, c, positions, w_ada, b_ada, norm_w, w_in, b_f, q_lora_norm_w,
           kv_lora_norm_w, w_uq, w_ukv, qn_nope_a, qn_rope_a, kn_nope_a,
           kn_rope_a, qn_b, kn_b, w_branch_a, w_branch_b, w_out):
    bsz, seq, d = x.shape
    depth = w_in.shape[0]
    tm = 512
    tq = 512
    pad = HEAD_PAD - NOPE_DIM - ROPE_DIM

    inv_freq = ROPE_THETA ** (-jnp.arange(0, ROPE_DIM, 2, dtype=F32) / ROPE_DIM)
    invf = jnp.concatenate([jnp.zeros((ROPE_LANE0,), F32), inv_freq, inv_freq,
                            jnp.zeros((pad,), F32)]).reshape(1, LANES)
    pos3 = positions.reshape(bsz, seq, 1)
    head = 2 * [(NOPE_DIM, True), (ROPE_DIM, True), (pad, False)]
    g_a = _block_avg(head, MXU_DIM)
    g_b = _block_avg(4 * [(V_DIM, True)], MXU_DIM)
    place = _bias_placement()
    sm_a = LOG2E / math.sqrt(NOPE_DIM + ROPE_DIM)
    sm_b = LOG2E / math.sqrt(V_DIM)

    for l in range(depth):
        ada3 = _ada_proj(c, w_ada[l], b_ada[l]).reshape(bsz, 3, d)

        wi = w_in[l]
        o = [0]
        for s in (Q_LORA, KV_LORA, ROPE_DIM, WIDTH, WIDTH, WIDTH, WIDTH, N_HEADS,
                  WIDTH, D_MODEL, D_MODEL):
            o.append(o[-1] + s)
        (w_cq, w_ckv, w_kr, w_ga, w_qb, w_kb, w_vb, w_f, w_gb, w_ma,
         w_mb) = [wi[:, o[n]:o[n + 1]] for n in range(11)]
        w_krf = jnp.concatenate(
            [w_f, jnp.zeros((d, ROPE_LANE0 - N_HEADS), F32), w_kr,
             jnp.zeros((d, pad), F32)], axis=1)
        win = jnp.concatenate([w_cq, w_ckv, w_krf, w_qb, w_kb, w_vb, w_ga, w_gb,
                               w_ma, w_mb], axis=1).astype(BF16)

        wuq = jnp.pad(w_uq[l].reshape(Q_LORA, N_HEADS, NOPE_DIM + ROPE_DIM),
                      ((0, 0), (0, 0), (0, pad))
                      ).reshape(Q_LORA, QK_WIDTH).astype(BF16)
        wkv = w_ukv[l].reshape(KV_LORA, N_HEADS, NOPE_DIM + V_DIM)
        wuk = jnp.pad(wkv[:, :, :NOPE_DIM], ((0, 0), (0, 0), (0, HEAD_PAD - NOPE_DIM))
                      ).reshape(KV_LORA, QK_WIDTH).astype(BF16)
        wuv = wkv[:, :, NOPE_DIM:].reshape(KV_LORA, WIDTH).astype(BF16)

        qaw = _tile_heads(jnp.concatenate([qn_nope_a[l], qn_rope_a[l]]) * sm_a, HEAD_PAD)
        kaw = _tile_heads(kn_nope_a[l], HEAD_PAD)
        krw = jnp.concatenate([jnp.zeros((ROPE_LANE0,), F32), kn_rope_a[l],
                               jnp.zeros((pad,), F32)]).reshape(1, LANES)
        qbw = _tile_heads(qn_b[l] * sm_b, V_DIM)
        kbw = _tile_heads(kn_b[l], V_DIM)
        bfv = jnp.pad(b_f[l].astype(F32), (0, LANES - N_HEADS)).reshape(1, LANES)

        consts = [norm_w[l].reshape(1, d), win, wuq, wuk, wuv, g_a, g_b,
                  q_lora_norm_w[l].reshape(1, Q_LORA),
                  kv_lora_norm_w[l].reshape(1, KV_LORA), qaw, kaw, krw, qbw, kbw,
                  invf, bfv, place]
        (qa, ka, vat, qb, kb, vbt, ga, gb, ma, mb) = _in_proj(x, ada3, pos3, consts, tm)

        ya, yb = _attention(qa, ka, vat, ga, qb, kb, vbt, gb, tq=tq)

        x = _out_proj(x, ada3, ya, yb, ma, mb, w_branch_a[l].astype(BF16),
                      w_branch_b[l].astype(BF16), w_out[l].astype(BF16), tm)
    return x
```

```python
import functools
import math

import jax
import jax.numpy as jnp
from jax import lax
from jax.experimental import pallas as pl
from jax.experimental.pallas import tpu as pltpu

D_MODEL = 1024
CHUNK = 64
N_HEADS = 8
NOPE_DIM = 64
ROPE_DIM = 32
V_DIM = 64
Q_LORA = 384
KV_LORA = 256
WIDTH = N_HEADS * V_DIM
ROPE_THETA = 10000.0
EPS = 1e-6
LOG2E = math.log2(math.e)

LANES = 128
HEAD_PAD = 128
QK_WIDTH = N_HEADS * HEAD_PAD
MXU_DIM = 256
BF16_ROWS = 16
VMEM_LIMIT = 48 * 1024 * 1024

C_CQ = 0
C_CKV = C_CQ + Q_LORA
C_KR = C_CKV + KV_LORA
C_QB = C_KR + LANES
C_KB = C_QB + WIDTH
C_VB = C_KB + WIDTH
C_GA = C_VB + WIDTH
C_GB = C_GA + WIDTH
C_MA = C_GB + WIDTH
C_MB = C_MA + D_MODEL
C_END = C_MB + D_MODEL
ROPE_LANE0 = NOPE_DIM
BIAS_LANE0 = V_DIM
BIAS_PARTS = 3
SHIFT_LANE0 = NOPE_DIM + ROPE_DIM
SHIFT_MAX = 48.0
BOUND_SLACK = 1.02

BF16 = jnp.bfloat16
F32 = jnp.float32


def _const_spec(shape):
    nd = len(shape)
    return pl.BlockSpec(shape, lambda *_: (0,) * nd, pipeline_mode=pl.Buffered(1))


def _ada_kernel(c_ref, w_ref, b_ref, o_ref):
    o_ref[...] = jnp.dot(c_ref[...].astype(BF16), w_ref[...].astype(BF16),
                         preferred_element_type=F32) + b_ref[...]


def _ada_proj(c, w_ada, b_ada):
    bsz, d = c.shape
    n = w_ada.shape[1]
    tn = 512
    return pl.pallas_call(
        _ada_kernel,
        grid=(n // tn,),
        in_specs=[pl.BlockSpec((bsz, d), lambda j: (0, 0)),
                  pl.BlockSpec((d, tn), lambda j: (0, j)),
                  pl.BlockSpec((1, tn), lambda j: (0, j))],
        out_specs=pl.BlockSpec((bsz, tn), lambda j: (0, j)),
        out_shape=jax.ShapeDtypeStruct((bsz, n), F32),
        name="ada_proj",
    )(c, w_ada, b_ada.reshape(1, n))


def _rms(x, n):
    return lax.rsqrt(jnp.sum(x * x, axis=-1, keepdims=True) * (1.0 / n) + EPS)


def _group_mean_sq(x, g_ref):
    g = g_ref[...]
    sq = (x * x).astype(BF16)
    parts = [jnp.dot(sq[:, c:c + MXU_DIM], g, preferred_element_type=F32)
             for c in range(0, x.shape[1], MXU_DIM)]
    return jnp.concatenate(parts, axis=1)


def _rot_half(x):
    lane = lax.broadcasted_iota(jnp.int32, x.shape, 1)
    half = ROPE_DIM // 2
    return jnp.where(lane < ROPE_LANE0 + half,
                     pltpu.roll(x, LANES - half, axis=1),
                     pltpu.roll(x, half, axis=1))


def _head_tiles(x):
    tiles = []
    for hd in range(N_HEADS):
        blk = x[:, (hd // 2) * LANES:(hd // 2 + 1) * LANES]
        tiles.append(pltpu.roll(blk, V_DIM, axis=1) if hd % 2 else blk)
    return tiles


def _in_proj_kernel(x_ref, ada_ref, pos_ref, nw_ref, win_ref, wuq_ref, wuk_ref,
                    wuv_ref, ga_mat_ref, gb_mat_ref, qlw_ref, kvlw_ref, qaw_ref,
                    kaw_ref, krw_ref, qbw_ref, kbw_ref, invf_ref, bf_ref, place_ref,
                    ga_shift_ref, gb_shift_ref, qa_ref, ka_ref, vat_ref, qb_ref, kb_ref, vbt_ref,
                    ga_ref, gb_ref, ma_ref, mb_ref, carry_ref):
    tm = x_ref.shape[1]
    x = x_ref[0]
    shift = ada_ref[0, 0:1, :]
    scale = ada_ref[0, 1:2, :]
    h = (x * _rms(x, D_MODEL)) * nw_ref[...] * (1.0 + scale) + shift
    hb = h.astype(BF16)

    def proj(lo, hi):
        return jnp.dot(hb, win_ref[:, lo:hi], preferred_element_type=F32)

    pieces = [(ga_ref, C_GA, C_GB), (gb_ref, C_GB, C_MA), (ma_ref, C_MA, C_MB),
              (mb_ref, C_MB, C_END)]

    def fill(n=1):
        for _ in range(min(n, len(pieces))):
            ref, lo, hi = pieces.pop(0)
            ref[0] = proj(lo, hi).astype(BF16)

    ang = invf_ref[...] * pos_ref[0].astype(F32)
    cos_h = jnp.cos(ang)
    sin_h = jnp.sin(ang)
    pad = HEAD_PAD - ROPE_LANE0 - ROPE_DIM
    cos_t = jnp.concatenate([jnp.ones((ROPE_LANE0, tm), F32), cos_h, cos_h,
                             jnp.ones((pad, tm), F32)], axis=0).T
    sin_t = jnp.concatenate([jnp.zeros((ROPE_LANE0, tm), F32), -sin_h, sin_h,
                             jnp.zeros((pad, tm), F32)], axis=0).T
    lane = lax.broadcasted_iota(jnp.int32, (tm, LANES), 1)

    def rope(t):
        return t * cos_t + _rot_half(t) * sin_t

    lat = proj(C_CQ, C_QB)
    cq = lat[:, C_CQ:C_CKV]
    ckv = lat[:, C_CKV:C_KR]
    kr = lat[:, C_KR:C_QB]
    fill(2)

    cqn = (cq * _rms(cq, Q_LORA) * qlw_ref[...]).astype(BF16)
    q = jnp.dot(cqn, wuq_ref[...], preferred_element_type=F32)
    qn = q * lax.rsqrt(_group_mean_sq(q, ga_mat_ref) + EPS) * qaw_ref[...]
    shift_lane = (lane >= SHIFT_LANE0) & (lane < SHIFT_LANE0 + BIAS_PARTS)
    ones_a = jnp.where(shift_lane, 1.0, 0.0)
    for hd in range(N_HEADS):
        sl = slice(hd * HEAD_PAD, (hd + 1) * HEAD_PAD)
        qa_ref[0, :, sl] = (rope(qn[:, sl]) + ones_a).astype(BF16)
    fill()

    ckvn = (ckv * _rms(ckv, KV_LORA) * kvlw_ref[...]).astype(BF16)
    k = jnp.dot(ckvn, wuk_ref[...], preferred_element_type=F32)
    kn = k * lax.rsqrt(_group_mean_sq(k, ga_mat_ref) + EPS) * kaw_ref[...]
    rope_lane = (lane >= ROPE_LANE0) & (lane < ROPE_LANE0 + ROPE_DIM)
    kr_ms = jnp.sum(jnp.where(rope_lane, kr * kr, 0.0), axis=-1, keepdims=True)
    krn = kr * lax.rsqrt(kr_ms * (1.0 / ROPE_DIM) + EPS) * krw_ref[...]
    krr = rope(krn) + ga_shift_ref[...]
    for hd in range(N_HEADS):
        sl = slice(hd * HEAD_PAD, (hd + 1) * HEAD_PAD)
        ka_ref[0, :, sl] = (kn[:, sl] + krr).astype(BF16)
    va = jnp.dot(ckvn, wuv_ref[...], preferred_element_type=F32)
    vat_ref[0] = va.T.astype(BF16)
    fill()

    z = kr + bf_ref[...]
    logf = jnp.minimum(z, 0.0) - jnp.log1p(jnp.exp(-jnp.abs(z)))
    c = jnp.where(lane < N_HEADS, logf, 0.0)
    row = lax.broadcasted_iota(jnp.int32, c.shape, 0)
    step = 1
    while step < tm:
        c = c + jnp.where(row >= step, pltpu.roll(c, step, axis=0), 0.0)
        step *= 2

    @pl.when(pl.program_id(1) == 0)
    def _():
        carry_ref[...] = jnp.zeros_like(carry_ref)

    c = c + carry_ref[...]
    carry_ref[...] = c[tm - 1:tm, :]
    c2 = c * LOG2E

    def parts(v):
        hi = v.astype(BF16).astype(F32)
        mid = (v - hi).astype(BF16).astype(F32)
        return hi, mid, v - hi - mid

    kparts = parts(-c2 - gb_shift_ref[...])
    qparts = parts(c2)
    packed = jnp.where(lane == 2 * BIAS_PARTS * N_HEADS, 1.0, 0.0)
    for n, part in enumerate(kparts + qparts):
        lo_lane = n * N_HEADS
        moved = part if n == 0 else pltpu.roll(part, lo_lane, axis=1)
        packed = jnp.where((lane >= lo_lane) & (lane < lo_lane + N_HEADS), moved, packed)
    bias = jnp.dot(packed.astype(BF16), place_ref[...], preferred_element_type=F32)

    head_lane = lane < V_DIM
    qb = proj(C_QB, C_KB)
    qbn = qb * lax.rsqrt(_group_mean_sq(qb, gb_mat_ref) + EPS) * qbw_ref[...]
    for hd, t in enumerate(_head_tiles(qbn)):
        sl = slice(hd * HEAD_PAD, (hd + 1) * HEAD_PAD)
        qb_ref[0, :, sl] = jnp.where(
            head_lane, t, bias[:, QK_WIDTH + hd * HEAD_PAD:QK_WIDTH + (hd + 1) * HEAD_PAD]
        ).astype(BF16)
    kb = proj(C_KB, C_VB)
    kbn = kb * lax.rsqrt(_group_mean_sq(kb, gb_mat_ref) + EPS) * kbw_ref[...]
    for hd, t in enumerate(_head_tiles(kbn)):
        sl = slice(hd * HEAD_PAD, (hd + 1) * HEAD_PAD)
        kb_ref[0, :, sl] = jnp.where(head_lane, t, bias[:, sl]).astype(BF16)
    vbt_ref[0] = proj(C_VB, C_GA).T.astype(BF16)
    assert not pieces


def _in_proj(x, ada3, pos3, consts, tm):
    bsz, seq, d = x.shape
    grid = (bsz, seq // tm)
    tok = lambda width: pl.BlockSpec((1, tm, width), lambda b, i: (b, i, 0))
    tok_t = pl.BlockSpec((1, WIDTH, tm), lambda b, i: (b, 0, i))
    in_specs = [tok(d),
                pl.BlockSpec((1, 3, d), lambda b, i: (b, 0, 0)),
                pl.BlockSpec((1, 1, tm), lambda b, i: (b, 0, i))]
    in_specs += [_const_spec(a.shape) for a in consts]
    row_major = lambda w: jax.ShapeDtypeStruct((bsz, seq, w), BF16)
    transposed = jax.ShapeDtypeStruct((bsz, WIDTH, seq), BF16)
    out_shape = [row_major(QK_WIDTH), row_major(QK_WIDTH), transposed,
                 row_major(QK_WIDTH), row_major(QK_WIDTH), transposed,
                 row_major(WIDTH), row_major(WIDTH), row_major(d), row_major(d)]
    out_specs = [tok(QK_WIDTH), tok(QK_WIDTH), tok_t,
                 tok(QK_WIDTH), tok(QK_WIDTH), tok_t,
                 tok(WIDTH), tok(WIDTH), tok(d), tok(d)]
    return pl.pallas_call(
        _in_proj_kernel,
        grid=grid,
        in_specs=in_specs,
        out_specs=out_specs,
        out_shape=out_shape,
        scratch_shapes=[pltpu.VMEM((1, LANES), F32)],
        compiler_params=pltpu.CompilerParams(
            dimension_semantics=("parallel", "arbitrary"),
            vmem_limit_bytes=VMEM_LIMIT),
        name="in_proj",
    )(x, ada3, pos3, *consts)


def _attn_kernel(flag_ref, qa_ref, ka_ref, vat_ref, ga_ref, qb_ref, kb_ref, vbt_ref,
                 gb_ref, oa_ref, ob_ref, *, tq, n_q_tiles):
    half = tq // 2
    nt = (((1,), (1,)), ((), ()))
    key = lax.broadcasted_iota(jnp.int32, (half, half), 0)
    qry = lax.broadcasted_iota(jnp.int32, (half, half), 1)
    ones = jnp.ones((BF16_ROWS, tq), BF16)
    branches = [(qa_ref, ka_ref, vat_ref, ga_ref, oa_ref, (key // CHUNK) <= (qry // CHUNK)),
                (qb_ref, kb_ref, vbt_ref, gb_ref, ob_ref, key <= qry)]
    chains = [(br, hh) for br in range(2) for hh in range(2)]

    def lanes(hh):
        return slice(hh * HEAD_PAD, (hh + 1) * HEAD_PAD)

    def values(br, hh, lo, hi):
        vt = branches[br][2][0, hh * V_DIM:(hh + 1) * V_DIM, lo:hi]
        return jnp.concatenate([vt, ones[:, :hi - lo]], axis=0)

    def probs(st, m, in_f32):
        if in_f32:
            return jnp.exp2(st).astype(BF16)
        return jnp.exp2((st - m).astype(BF16))

    def full_scores(br, hh, t, j):
        q_ref, k_ref = branches[br][:2]
        st = lax.dot_general(k_ref[0, j * tq:(j + 1) * tq, lanes(hh)],
                             q_ref[0, t * tq:(t + 1) * tq, lanes(hh)], nt,
                             preferred_element_type=F32)
        return st, jnp.max(st, axis=0, keepdims=True)

    def full_pv(br, hh, j, st, m, in_f32=False):
        return jnp.dot(values(br, hh, j * tq, (j + 1) * tq), probs(st, m, in_f32),
                       preferred_element_type=F32)

    def diag_scores(br, hh, t):
        q_ref, k_ref = branches[br][:2]
        mask = branches[br][5]
        base = t * tq
        st_lo = lax.dot_general(k_ref[0, base:base + half, lanes(hh)],
                                q_ref[0, base:base + tq, lanes(hh)], nt,
                                preferred_element_type=F32)
        st_hi = lax.dot_general(k_ref[0, base + half:base + tq, lanes(hh)],
                                q_ref[0, base + half:base + tq, lanes(hh)], nt,
                                preferred_element_type=F32)
        st_lo = jnp.concatenate(
            [jnp.where(mask, st_lo[:, :half], -jnp.inf), st_lo[:, half:]], axis=1)
        st_hi = jnp.where(mask, st_hi, -jnp.inf)
        mx_lo = jnp.max(st_lo, axis=0, keepdims=True)
        mx_hi = jnp.max(st_hi, axis=0, keepdims=True)
        mx = jnp.concatenate(
            [mx_lo[:, :half], jnp.maximum(mx_lo[:, half:], mx_hi)], axis=1)
        return (st_lo, st_hi), mx

    def diag_pv(br, hh, t, st, m, in_f32=False):
        st_lo, st_hi = st
        base = t * tq
        pv = jnp.dot(values(br, hh, base, base + half), probs(st_lo, m, in_f32),
                     preferred_element_type=F32)
        pv_hi = jnp.dot(values(br, hh, base + half, base + tq),
                        probs(st_hi, m[:, half:], in_f32), preferred_element_type=F32)
        return jnp.concatenate([pv[:, :half], pv[:, half:] + pv_hi], axis=1)

    def scores(br, hh, t, j):
        return diag_scores(br, hh, t) if j == t else full_scores(br, hh, t, j)

    def update(br, hh, t, j, st, mx, state):
        m_new = mx if state is None else jnp.maximum(state[0], mx)
        pv = diag_pv(br, hh, t, st, m_new) if j == t else full_pv(br, hh, j, st, m_new)
        if state is None:
            return m_new, pv
        m, acc = state
        return m_new, jnp.exp2(m - m_new) * acc + pv

    def finish(t, accs):
        for br in range(2):
            g_ref, o_ref = branches[br][3:5]
            halves = [accs[(br, hh)] for hh in range(2)]
            halves = [acc[:V_DIM] / acc[V_DIM:V_DIM + 1] for acc in halves]
            o = jnp.concatenate(halves, axis=0).T
            g = g_ref[0, t * tq:(t + 1) * tq, :].astype(F32)
            o_ref[0, t * tq:(t + 1) * tq, :] = (o * (g * jax.nn.sigmoid(g))).astype(BF16)

    def shifted_path():
        zero = jnp.zeros((1, tq), F32)
        for t in range(n_q_tiles):
            sts = {c: diag_scores(*c, t)[0] for c in chains}
            accs = {}
            for j in range(t + 1):
                nxt = {c: full_scores(*c, t, j)[0] for c in chains} if j < t else None
                for c in chains:
                    if j == 0:
                        accs[c] = diag_pv(*c, t, sts[c], zero, in_f32=True)
                    else:
                        accs[c] = accs[c] + full_pv(*c, j - 1, sts[c], zero, in_f32=True)
                sts = nxt
            finish(t, accs)

    def online_path():
        for t in range(n_q_tiles):
            state = {c: None for c in chains}
            nxt = {c: scores(*c, t, 0) for c in chains}
            for j in range(t + 1):
                cur = nxt
                if j < t:
                    nxt = {c: scores(*c, t, j + 1) for c in chains}
                state = {c: update(*c, t, j, *cur[c], state[c]) for c in chains}
            finish(t, {c: state[c][1] for c in chains})

    pl.when(flag_ref[0] != 0)(shifted_path)
    pl.when(flag_ref[0] == 0)(online_path)


def _attention(flag, qa, ka, vat, ga, qb, kb, vbt, gb, *, tq):
    bsz, seq, _ = qa.shape
    pairs = N_HEADS // 2
    kern = functools.partial(_attn_kernel, tq=tq, n_q_tiles=seq // tq)
    qk_spec = pl.BlockSpec((1, seq, 2 * HEAD_PAD), lambda b, p: (b, 0, p))
    vt_spec = pl.BlockSpec((1, 2 * V_DIM, seq), lambda b, p: (b, p, 0))
    tok_spec = pl.BlockSpec((1, seq, 2 * V_DIM), lambda b, p: (b, 0, p))
    out = jax.ShapeDtypeStruct((bsz, seq, WIDTH), BF16)
    return pl.pallas_call(
        kern,
        grid=(bsz, pairs),
        in_specs=[pl.BlockSpec(memory_space=pltpu.SMEM)]
        + 2 * [qk_spec, qk_spec, vt_spec, tok_spec],
        out_specs=[tok_spec, tok_spec],
        out_shape=[out, out],
        compiler_params=pltpu.CompilerParams(
            dimension_semantics=("parallel", "arbitrary"),
            vmem_limit_bytes=VMEM_LIMIT),
        name="attn",
    )(flag, qa, ka, vat, ga, qb, kb, vbt, gb)


def _out_kernel(x_ref, ada_ref, ya_ref, yb_ref, ma_ref, mb_ref, wa_ref, wb_ref,
                wo_ref, o_ref):
    pa = jnp.dot(ya_ref[0], wa_ref[...], preferred_element_type=F32)
    pb = jnp.dot(yb_ref[0], wb_ref[...], preferred_element_type=F32)
    merged = (jax.nn.sigmoid(ma_ref[0].astype(F32)) * pa
              + jax.nn.sigmoid(mb_ref[0].astype(F32)) * pb)
    upd = jnp.dot(merged.astype(BF16), wo_ref[...], preferred_element_type=F32)
    o_ref[0] = x_ref[0] + ada_ref[0, 2:3, :] * upd


def _out_proj(x, ada3, ya, yb, ma, mb, wa, wb, wo, tm):
    bsz, seq, d = x.shape
    tok = lambda width: pl.BlockSpec((1, tm, width), lambda b, i: (b, i, 0))
    return pl.pallas_call(
        _out_kernel,
        grid=(bsz, seq // tm),
        in_specs=[tok(d), pl.BlockSpec((1, 3, d), lambda b, i: (b, 0, 0)),
                  tok(WIDTH), tok(WIDTH), tok(d), tok(d),
                  _const_spec(wa.shape), _const_spec(wb.shape), _const_spec(wo.shape)],
        out_specs=tok(d),
        out_shape=jax.ShapeDtypeStruct((bsz, seq, d), x.dtype),
        compiler_params=pltpu.CompilerParams(
            dimension_semantics=("parallel", "arbitrary"),
            vmem_limit_bytes=VMEM_LIMIT),
        name="out_proj",
    )(x, ada3, ya, yb, ma, mb, wa, wb, wo)


def _tile_heads(vec, pad_to):
    v = jnp.pad(vec.astype(F32), (0, pad_to - vec.shape[0]))
    return jnp.tile(v, N_HEADS).reshape(1, N_HEADS * pad_to)


def _block_avg(sizes, total):
    idx = jnp.arange(total)
    gid = jnp.full((total,), -1, jnp.int32)
    inv = jnp.zeros((total,), F32)
    start = 0
    for n, (size, live) in enumerate(sizes):
        inside = (idx >= start) & (idx < start + size)
        if live:
            gid = jnp.where(inside, n, gid)
            inv = jnp.where(inside, 1.0 / size, inv)
        start += size
    same = (gid[:, None] == gid[None, :]) & (gid[:, None] >= 0)
    return jnp.where(same, inv[None, :], 0.0).astype(BF16)


def _bias_placement():
    src = jnp.arange(LANES)[:, None]
    dst = jnp.arange(2 * QK_WIDTH)[None, :]
    is_q_tile = dst >= QK_WIDTH
    tile_head = (dst % QK_WIDTH) // HEAD_PAD
    tile_lane = dst % HEAD_PAD
    group, head = src // N_HEADS, src % N_HEADS
    k_col = (tile_lane >= BIAS_LANE0) & (tile_lane < BIAS_LANE0 + BIAS_PARTS)
    q_col = (tile_lane >= BIAS_LANE0 + BIAS_PARTS) & (tile_lane < BIAS_LANE0 + 2 * BIAS_PARTS)
    k_part = (group < BIAS_PARTS) & ~is_q_tile & (tile_head == head) \
        & (tile_lane == BIAS_LANE0 + group)
    q_part = (group >= BIAS_PARTS) & (group < 2 * BIAS_PARTS) & is_q_tile \
        & (tile_head == head) & (tile_lane == BIAS_LANE0 + group)
    one = (src == 2 * BIAS_PARTS * N_HEADS) & jnp.where(is_q_tile, k_col, q_col)
    return (k_part | q_part | one).astype(BF16)


def _shift_parts(g):
    hi = (-g).astype(BF16).astype(F32)
    mid = (-g - hi).astype(BF16).astype(F32)
    return hi, mid, -g - hi - mid


def kernel(x, c, positions, w_ada, b_ada, norm_w, w_in, b_f, q_lora_norm_w,
           kv_lora_norm_w, w_uq, w_ukv, qn_nope_a, qn_rope_a, kn_nope_a,
           kn_rope_a, qn_b, kn_b, w_branch_a, w_branch_b, w_out):
    bsz, seq, d = x.shape
    depth = w_in.shape[0]
    tm = 512
    tq = 512
    pad = HEAD_PAD - NOPE_DIM - ROPE_DIM

    inv_freq = ROPE_THETA ** (-jnp.arange(0, ROPE_DIM, 2, dtype=F32) / ROPE_DIM)
    invf = inv_freq.reshape(ROPE_DIM // 2, 1)
    pos3 = positions.reshape(bsz, 1, seq)
    head = 2 * [(NOPE_DIM, True), (ROPE_DIM, True), (pad, False)]
    g_a = _block_avg(head, MXU_DIM)
    g_b = _block_avg(4 * [(V_DIM, True)], MXU_DIM)
    place = _bias_placement()
    sm_a = LOG2E / math.sqrt(NOPE_DIM + ROPE_DIM)
    sm_b = LOG2E / math.sqrt(V_DIM)

    for l in range(depth):
        ada3 = _ada_proj(c, w_ada[l], b_ada[l]).reshape(bsz, 3, d)

        wi = w_in[l]
        o = [0]
        for s in (Q_LORA, KV_LORA, ROPE_DIM, WIDTH, WIDTH, WIDTH, WIDTH, N_HEADS,
                  WIDTH, D_MODEL, D_MODEL):
            o.append(o[-1] + s)
        (w_cq, w_ckv, w_kr, w_ga, w_qb, w_kb, w_vb, w_f, w_gb, w_ma,
         w_mb) = [wi[:, o[n]:o[n + 1]] for n in range(11)]
        w_krf = jnp.concatenate(
            [w_f, jnp.zeros((d, ROPE_LANE0 - N_HEADS), F32), w_kr,
             jnp.zeros((d, pad), F32)], axis=1)
        win = jnp.concatenate([w_cq, w_ckv, w_krf, w_qb, w_kb, w_vb, w_ga, w_gb,
                               w_ma, w_mb], axis=1).astype(BF16)

        wuq = jnp.pad(w_uq[l].reshape(Q_LORA, N_HEADS, NOPE_DIM + ROPE_DIM),
                      ((0, 0), (0, 0), (0, pad))
                      ).reshape(Q_LORA, QK_WIDTH).astype(BF16)
        wkv = w_ukv[l].reshape(KV_LORA, N_HEADS, NOPE_DIM + V_DIM)
        wuk = jnp.pad(wkv[:, :, :NOPE_DIM], ((0, 0), (0, 0), (0, HEAD_PAD - NOPE_DIM))
                      ).reshape(KV_LORA, QK_WIDTH).astype(BF16)
        wuv = wkv[:, :, NOPE_DIM:].reshape(KV_LORA, WIDTH).astype(BF16)

        qaw = _tile_heads(jnp.concatenate([qn_nope_a[l], qn_rope_a[l]]) * sm_a, HEAD_PAD)
        kaw = _tile_heads(kn_nope_a[l], HEAD_PAD)
        krw = jnp.concatenate([jnp.zeros((ROPE_LANE0,), F32), kn_rope_a[l],
                               jnp.zeros((pad,), F32)]).reshape(1, LANES)
        qbw = _tile_heads(qn_b[l] * sm_b, V_DIM)
        kbw = _tile_heads(kn_b[l], V_DIM)
        bfv = jnp.pad(b_f[l].astype(F32), (0, LANES - N_HEADS)).reshape(1, LANES)

        amax = lambda v: jnp.max(jnp.abs(v.astype(F32)))
        bound_a = sm_a * BOUND_SLACK * jnp.sqrt(
            (NOPE_DIM * amax(qn_nope_a[l]) ** 2 + ROPE_DIM * amax(qn_rope_a[l]) ** 2)
            * (NOPE_DIM * amax(kn_nope_a[l]) ** 2 + ROPE_DIM * amax(kn_rope_a[l]) ** 2))
        bound_b = sm_b * BOUND_SLACK * V_DIM * amax(qn_b[l]) * amax(kn_b[l])
        bounded = jnp.maximum(bound_a, bound_b) <= SHIFT_MAX
        shift_a = jnp.where(bounded, bound_a, 0.0)
        shift_b = jnp.where(bounded, bound_b, 0.0)
        ga_shift = jnp.zeros((LANES,), F32).at[SHIFT_LANE0:SHIFT_LANE0 + BIAS_PARTS].set(
            jnp.stack(_shift_parts(shift_a))).reshape(1, LANES)
        gb_shift = jnp.full((1, LANES), shift_b, F32)
        flag = bounded.astype(jnp.int32).reshape(1)

        consts = [norm_w[l].reshape(1, d), win, wuq, wuk, wuv, g_a, g_b,
                  q_lora_norm_w[l].reshape(1, Q_LORA),
                  kv_lora_norm_w[l].reshape(1, KV_LORA), qaw, kaw, krw, qbw, kbw,
                  invf, bfv, place, ga_shift, gb_shift]
        (qa, ka, vat, qb, kb, vbt, ga, gb, ma, mb) = _in_proj(x, ada3, pos3, consts, tm)

        ya, yb = _attention(flag, qa, ka, vat, ga, qb, kb, vbt, gb, tq=tq)

        x = _out_proj(x, ada3, ya, yb, ma, mb, w_branch_a[l].astype(BF16),
                      w_branch_b[l].astype(BF16), w_out[l].astype(BF16), tm)
    return x
```

```python
import functools
import math

import jax
import jax.numpy as jnp
from jax import lax
from jax.experimental import pallas as pl
from jax.experimental.pallas import tpu as pltpu

D_MODEL = 1024
CHUNK = 64
N_HEADS = 8
NOPE_DIM = 64
ROPE_DIM = 32
V_DIM = 64
Q_LORA = 384
KV_LORA = 256
WIDTH = N_HEADS * V_DIM
ROPE_THETA = 10000.0
EPS = 1e-6
LOG2E = math.log2(math.e)

LANES = 128
HEAD_PAD = 128
QK_WIDTH = N_HEADS * HEAD_PAD
MXU_DIM = 256
BF16_ROWS = 16
VMEM_LIMIT = 48 * 1024 * 1024

C_CQ = 0
C_CKV = C_CQ + Q_LORA
C_KR = C_CKV + KV_LORA
C_QB = C_KR + LANES
C_KB = C_QB + WIDTH
C_VB = C_KB + WIDTH
C_GA = C_VB + WIDTH
C_GB = C_GA + WIDTH
C_MA = C_GB + WIDTH
C_MB = C_MA + D_MODEL
C_END = C_MB + D_MODEL
ROPE_LANE0 = NOPE_DIM
BIAS_LANE0 = V_DIM
BIAS_PARTS = 3
SHIFT_LANE0 = NOPE_DIM + ROPE_DIM
SHIFT_MAX = 48.0
BOUND_SLACK = 1.02

BF16 = jnp.bfloat16
F32 = jnp.float32


def _const_spec(shape):
    nd = len(shape)
    return pl.BlockSpec(shape, lambda *_: (0,) * nd, pipeline_mode=pl.Buffered(1))


def _ada_kernel(c_ref, w_ref, b_ref, o_ref):
    o_ref[...] = jnp.dot(c_ref[...].astype(BF16), w_ref[...].astype(BF16),
                         preferred_element_type=F32) + b_ref[...]


def _ada_proj(c, w_ada, b_ada):
    bsz, d = c.shape
    n = w_ada.shape[1]
    tn = 512
    return pl.pallas_call(
        _ada_kernel,
        grid=(n // tn,),
        in_specs=[pl.BlockSpec((bsz, d), lambda j: (0, 0)),
                  pl.BlockSpec((d, tn), lambda j: (0, j)),
                  pl.BlockSpec((1, tn), lambda j: (0, j))],
        out_specs=pl.BlockSpec((bsz, tn), lambda j: (0, j)),
        out_shape=jax.ShapeDtypeStruct((bsz, n), F32),
        name="ada_proj",
    )(c, w_ada, b_ada.reshape(1, n))


def _rms(x, n):
    return lax.rsqrt(jnp.sum(x * x, axis=-1, keepdims=True) * (1.0 / n) + EPS)


def _group_mean_sq(x, g_ref):
    g = g_ref[...]
    sq = (x * x).astype(BF16)
    parts = [jnp.dot(sq[:, c:c + MXU_DIM], g, preferred_element_type=F32)
             for c in range(0, x.shape[1], MXU_DIM)]
    return jnp.concatenate(parts, axis=1)


def _rot_half(x):
    lane = lax.broadcasted_iota(jnp.int32, x.shape, 1)
    half = ROPE_DIM // 2
    return jnp.where(lane < ROPE_LANE0 + half,
                     pltpu.roll(x, LANES - half, axis=1),
                     pltpu.roll(x, half, axis=1))


def _head_tiles(x):
    tiles = []
    for hd in range(N_HEADS):
        blk = x[:, (hd // 2) * LANES:(hd // 2 + 1) * LANES]
        tiles.append(pltpu.roll(blk, V_DIM, axis=1) if hd % 2 else blk)
    return tiles


def _in_proj_kernel(x_ref, ada_ref, pos_ref, nw_ref, win_ref, wuq_ref, wuk_ref,
                    wuv_ref, ga_mat_ref, gb_mat_ref, qlw_ref, kvlw_ref, qaw_ref,
                    kaw_ref, krw_ref, qbw_ref, kbw_ref, invf_ref, bf_ref,
                    ga_shift_ref, gb_shift_ref, qa_ref, ka_ref, vat_ref, qb_ref, kb_ref, vbt_ref,
                    ga_ref, gb_ref, ma_ref, mb_ref, carry_ref):
    tm = x_ref.shape[1]
    x = x_ref[0]
    shift = ada_ref[0, 0:1, :]
    scale = ada_ref[0, 1:2, :]
    h = (x * _rms(x, D_MODEL)) * nw_ref[...] * (1.0 + scale) + shift
    hb = h.astype(BF16)

    def proj(lo, hi):
        return jnp.dot(hb, win_ref[:, lo:hi], preferred_element_type=F32)

    pieces = [(ga_ref, C_GA, C_GB), (gb_ref, C_GB, C_MA), (ma_ref, C_MA, C_MB),
              (mb_ref, C_MB, C_END)]

    def fill(n=1):
        for _ in range(min(n, len(pieces))):
            ref, lo, hi = pieces.pop(0)
            ref[0] = proj(lo, hi).astype(BF16)

    ang = invf_ref[...] * pos_ref[0].astype(F32)
    cos_h = jnp.cos(ang)
    sin_h = jnp.sin(ang)
    pad = HEAD_PAD - ROPE_LANE0 - ROPE_DIM
    cos_t = jnp.concatenate([jnp.ones((ROPE_LANE0, tm), F32), cos_h, cos_h,
                             jnp.ones((pad, tm), F32)], axis=0).T
    sin_t = jnp.concatenate([jnp.zeros((ROPE_LANE0, tm), F32), -sin_h, sin_h,
                             jnp.zeros((pad, tm), F32)], axis=0).T
    lane = lax.broadcasted_iota(jnp.int32, (tm, LANES), 1)

    def rope(t):
        return t * cos_t + _rot_half(t) * sin_t

    lat = proj(C_CQ, C_QB)
    cq = lat[:, C_CQ:C_CKV]
    ckv = lat[:, C_CKV:C_KR]
    kr = lat[:, C_KR:C_QB]
    fill(2)

    cqn = (cq * _rms(cq, Q_LORA) * qlw_ref[...]).astype(BF16)
    q = jnp.dot(cqn, wuq_ref[...], preferred_element_type=F32)
    qn = q * lax.rsqrt(_group_mean_sq(q, ga_mat_ref) + EPS) * qaw_ref[...]
    shift_lane = (lane >= SHIFT_LANE0) & (lane < SHIFT_LANE0 + BIAS_PARTS)
    ones_a = jnp.where(shift_lane, 1.0, 0.0)
    for hd in range(N_HEADS):
        sl = slice(hd * HEAD_PAD, (hd + 1) * HEAD_PAD)
        qa_ref[0, :, sl] = (rope(qn[:, sl]) + ones_a).astype(BF16)
    fill()

    ckvn = (ckv * _rms(ckv, KV_LORA) * kvlw_ref[...]).astype(BF16)
    k = jnp.dot(ckvn, wuk_ref[...], preferred_element_type=F32)
    kn = k * lax.rsqrt(_group_mean_sq(k, ga_mat_ref) + EPS) * kaw_ref[...]
    rope_lane = (lane >= ROPE_LANE0) & (lane < ROPE_LANE0 + ROPE_DIM)
    kr_ms = jnp.sum(jnp.where(rope_lane, kr * kr, 0.0), axis=-1, keepdims=True)
    krn = kr * lax.rsqrt(kr_ms * (1.0 / ROPE_DIM) + EPS) * krw_ref[...]
    krr = rope(krn) + ga_shift_ref[...]
    for hd in range(N_HEADS):
        sl = slice(hd * HEAD_PAD, (hd + 1) * HEAD_PAD)
        ka_ref[0, :, sl] = (kn[:, sl] + krr).astype(BF16)
    va = jnp.dot(ckvn, wuv_ref[...], preferred_element_type=F32)
    vat_ref[0] = va.T.astype(BF16)
    fill()

    z = kr + bf_ref[...]
    logf = jnp.minimum(z, 0.0) - jnp.log1p(jnp.exp(-jnp.abs(z)))
    c = jnp.where(lane < N_HEADS, logf, 0.0)
    row = lax.broadcasted_iota(jnp.int32, c.shape, 0)
    step = 1
    while step < tm:
        c = c + jnp.where(row >= step, pltpu.roll(c, step, axis=0), 0.0)
        step *= 2

    @pl.when(pl.program_id(1) == 0)
    def _():
        carry_ref[...] = jnp.zeros_like(carry_ref)

    c = c + carry_ref[...]
    carry_ref[...] = c[tm - 1:tm, :]
    c2 = c * LOG2E

    def parts(v):
        hi = v.astype(BF16).astype(F32)
        mid = (v - hi).astype(BF16).astype(F32)
        return hi, mid, v - hi - mid

    shared = jnp.zeros_like(c2)
    for n, part in enumerate(parts(-c2 - gb_shift_ref[...]) + parts(c2)):
        lo_lane = BIAS_LANE0 + n * N_HEADS
        shared = jnp.where((lane >= lo_lane) & (lane < lo_lane + N_HEADS),
                           pltpu.roll(part, lo_lane, axis=1), shared)
    q_part_lane0 = BIAS_LANE0 + BIAS_PARTS * N_HEADS
    in_k_parts = (lane >= BIAS_LANE0) & (lane < q_part_lane0)
    in_q_parts = (lane >= q_part_lane0) & (lane < q_part_lane0 + BIAS_PARTS * N_HEADS)
    k_tail = jnp.where(in_k_parts, shared, 0.0)
    q_tail = jnp.where(in_q_parts, shared, 0.0)
    lane_row = lane[:1]
    own = [(lane_row & (N_HEADS - 1)) == hd for hd in range(N_HEADS)]
    ones_q = [jnp.where(in_k_parts[:1] & o, 1.0, 0.0) for o in own]
    ones_k = [jnp.where(in_q_parts[:1] & o, 1.0, 0.0) for o in own]

    head_lane = lane < V_DIM
    qb = proj(C_QB, C_KB)
    qbn = qb * lax.rsqrt(_group_mean_sq(qb, gb_mat_ref) + EPS) * qbw_ref[...]
    for hd, t in enumerate(_head_tiles(qbn)):
        qb_ref[0, :, hd * HEAD_PAD:(hd + 1) * HEAD_PAD] = jnp.where(
            head_lane, t, q_tail + ones_q[hd]).astype(BF16)
    kb = proj(C_KB, C_VB)
    kbn = kb * lax.rsqrt(_group_mean_sq(kb, gb_mat_ref) + EPS) * kbw_ref[...]
    for hd, t in enumerate(_head_tiles(kbn)):
        kb_ref[0, :, hd * HEAD_PAD:(hd + 1) * HEAD_PAD] = jnp.where(
            head_lane, t, k_tail + ones_k[hd]).astype(BF16)
    vbt_ref[0] = proj(C_VB, C_GA).T.astype(BF16)
    assert not pieces


def _in_proj(x, ada3, pos3, consts, tm):
    bsz, seq, d = x.shape
    grid = (bsz, seq // tm)
    tok = lambda width: pl.BlockSpec((1, tm, width), lambda b, i: (b, i, 0))
    tok_t = pl.BlockSpec((1, WIDTH, tm), lambda b, i: (b, 0, i))
    in_specs = [tok(d),
                pl.BlockSpec((1, 3, d), lambda b, i: (b, 0, 0)),
                pl.BlockSpec((1, 1, tm), lambda b, i: (b, 0, i))]
    in_specs += [_const_spec(a.shape) for a in consts]
    row_major = lambda w: jax.ShapeDtypeStruct((bsz, seq, w), BF16)
    transposed = jax.ShapeDtypeStruct((bsz, WIDTH, seq), BF16)
    out_shape = [row_major(QK_WIDTH), row_major(QK_WIDTH), transposed,
                 row_major(QK_WIDTH), row_major(QK_WIDTH), transposed,
                 row_major(WIDTH), row_major(WIDTH), row_major(d), row_major(d)]
    out_specs = [tok(QK_WIDTH), tok(QK_WIDTH), tok_t,
                 tok(QK_WIDTH), tok(QK_WIDTH), tok_t,
                 tok(WIDTH), tok(WIDTH), tok(d), tok(d)]
    return pl.pallas_call(
        _in_proj_kernel,
        grid=grid,
        in_specs=in_specs,
        out_specs=out_specs,
        out_shape=out_shape,
        scratch_shapes=[pltpu.VMEM((1, LANES), F32)],
        compiler_params=pltpu.CompilerParams(
            dimension_semantics=("parallel", "arbitrary"),
            vmem_limit_bytes=VMEM_LIMIT),
        name="in_proj",
    )(x, ada3, pos3, *consts)


def _attn_kernel(flag_ref, qa_ref, ka_ref, vat_ref, ga_ref, qb_ref, kb_ref, vbt_ref,
                 gb_ref, oa_ref, ob_ref, *, tq, n_q_tiles):
    half = tq // 2
    nt = (((1,), (1,)), ((), ()))
    key = lax.broadcasted_iota(jnp.int32, (half, half), 0)
    qry = lax.broadcasted_iota(jnp.int32, (half, half), 1)
    ones = jnp.ones((BF16_ROWS, n_q_tiles * tq), BF16)
    branches = [(qa_ref, ka_ref, vat_ref, ga_ref, oa_ref, (key // CHUNK) <= (qry // CHUNK)),
                (qb_ref, kb_ref, vbt_ref, gb_ref, ob_ref, key <= qry)]
    chains = [(br, hh) for br in range(2) for hh in range(2)]

    def lanes(hh):
        return slice(hh * HEAD_PAD, (hh + 1) * HEAD_PAD)

    def values(br, hh, lo, hi):
        vt = branches[br][2][0, hh * V_DIM:(hh + 1) * V_DIM, lo:hi]
        return jnp.concatenate([vt, ones[:, :hi - lo]], axis=0)

    def probs(st, m, in_f32):
        if in_f32:
            return jnp.exp2(st).astype(BF16)
        return jnp.exp2((st - m).astype(BF16))

    def full_scores(br, hh, t, j):
        q_ref, k_ref = branches[br][:2]
        st = lax.dot_general(k_ref[0, j * tq:(j + 1) * tq, lanes(hh)],
                             q_ref[0, t * tq:(t + 1) * tq, lanes(hh)], nt,
                             preferred_element_type=F32)
        return st, jnp.max(st, axis=0, keepdims=True)

    def full_pv(br, hh, j, st, m, in_f32=False):
        return jnp.dot(values(br, hh, j * tq, (j + 1) * tq), probs(st, m, in_f32),
                       preferred_element_type=F32)

    def diag_scores(br, hh, t):
        q_ref, k_ref = branches[br][:2]
        mask = branches[br][5]
        base = t * tq
        st_lo = lax.dot_general(k_ref[0, base:base + half, lanes(hh)],
                                q_ref[0, base:base + tq, lanes(hh)], nt,
                                preferred_element_type=F32)
        st_hi = lax.dot_general(k_ref[0, base + half:base + tq, lanes(hh)],
                                q_ref[0, base + half:base + tq, lanes(hh)], nt,
                                preferred_element_type=F32)
        st_lo = jnp.concatenate(
            [jnp.where(mask, st_lo[:, :half], -jnp.inf), st_lo[:, half:]], axis=1)
        st_hi = jnp.where(mask, st_hi, -jnp.inf)
        mx_lo = jnp.max(st_lo, axis=0, keepdims=True)
        mx_hi = jnp.max(st_hi, axis=0, keepdims=True)
        mx = jnp.concatenate(
            [mx_lo[:, :half], jnp.maximum(mx_lo[:, half:], mx_hi)], axis=1)
        return (st_lo, st_hi), mx

    def diag_pv(br, hh, t, st, m, in_f32=False):
        st_lo, st_hi = st
        base = t * tq
        pv = jnp.dot(values(br, hh, base, base + half), probs(st_lo, m, in_f32),
                     preferred_element_type=F32)
        pv_hi = jnp.dot(values(br, hh, base + half, base + tq),
                        probs(st_hi, m[:, half:], in_f32), preferred_element_type=F32)
        return jnp.concatenate([pv[:, :half], pv[:, half:] + pv_hi], axis=1)

    def scores(br, hh, t, j):
        return diag_scores(br, hh, t) if j == t else full_scores(br, hh, t, j)

    def update(br, hh, t, j, st, mx, state):
        m_new = mx if state is None else jnp.maximum(state[0], mx)
        pv = diag_pv(br, hh, t, st, m_new) if j == t else full_pv(br, hh, j, st, m_new)
        if state is None:
            return m_new, pv
        m, acc = state
        return m_new, jnp.exp2(m - m_new) * acc + pv

    def finish(t, accs):
        for br in range(2):
            g_ref, o_ref = branches[br][3:5]
            halves = [accs[(br, hh)] for hh in range(2)]
            halves = [acc[:V_DIM] / acc[V_DIM:V_DIM + 1] for acc in halves]
            o = jnp.concatenate(halves, axis=0).T
            g = g_ref[0, t * tq:(t + 1) * tq, :].astype(F32)
            o_ref[0, t * tq:(t + 1) * tq, :] = (o * (g * jax.nn.sigmoid(g))).astype(BF16)

    def shifted_path():
        zero = jnp.zeros((1, tq), F32)

        def past_scores(br, hh, t):
            q_ref, k_ref = branches[br][:2]
            return lax.dot_general(k_ref[0, :t * tq, lanes(hh)],
                                   q_ref[0, t * tq:(t + 1) * tq, lanes(hh)], nt,
                                   preferred_element_type=F32)

        for t in range(n_q_tiles):
            diag = {c: diag_scores(*c, t)[0] for c in chains}
            past = {c: past_scores(*c, t) for c in chains} if t else None
            accs = {}
            for c in chains:
                acc = diag_pv(*c, t, diag[c], zero, in_f32=True)
                if t:
                    acc = acc + jnp.dot(values(*c, 0, t * tq), probs(past[c], zero, True),
                                        preferred_element_type=F32)
                accs[c] = acc
            finish(t, accs)

    def online_path():
        for t in range(n_q_tiles):
            state = {c: None for c in chains}
            nxt = {c: scores(*c, t, 0) for c in chains}
            for j in range(t + 1):
                cur = nxt
                if j < t:
                    nxt = {c: scores(*c, t, j + 1) for c in chains}
                state = {c: update(*c, t, j, *cur[c], state[c]) for c in chains}
            finish(t, {c: state[c][1] for c in chains})

    pl.when(flag_ref[0] != 0)(shifted_path)
    pl.when(flag_ref[0] == 0)(online_path)


def _attention(flag, qa, ka, vat, ga, qb, kb, vbt, gb, *, tq):
    bsz, seq, _ = qa.shape
    pairs = N_HEADS // 2
    kern = functools.partial(_attn_kernel, tq=tq, n_q_tiles=seq // tq)
    qk_spec = pl.BlockSpec((1, seq, 2 * HEAD_PAD), lambda b, p: (b, 0, p))
    vt_spec = pl.BlockSpec((1, 2 * V_DIM, seq), lambda b, p: (b, p, 0))
    tok_spec = pl.BlockSpec((1, seq, 2 * V_DIM), lambda b, p: (b, 0, p))
    out = jax.ShapeDtypeStruct((bsz, seq, WIDTH), BF16)
    return pl.pallas_call(
        kern,
        grid=(bsz, pairs),
        in_specs=[pl.BlockSpec(memory_space=pltpu.SMEM)]
        + 2 * [qk_spec, qk_spec, vt_spec, tok_spec],
        out_specs=[tok_spec, tok_spec],
        out_shape=[out, out],
        compiler_params=pltpu.CompilerParams(
            dimension_semantics=("parallel", "arbitrary"),
            vmem_limit_bytes=VMEM_LIMIT),
        name="attn",
    )(flag, qa, ka, vat, ga, qb, kb, vbt, gb)


def _out_kernel(x_ref, ada_ref, ya_ref, yb_ref, ma_ref, mb_ref, wa_ref, wb_ref,
                wo_ref, o_ref):
    pa = jnp.dot(ya_ref[0], wa_ref[...], preferred_element_type=F32)
    pb = jnp.dot(yb_ref[0], wb_ref[...], preferred_element_type=F32)
    merged = (jax.nn.sigmoid(ma_ref[0].astype(F32)) * pa
              + jax.nn.sigmoid(mb_ref[0].astype(F32)) * pb)
    upd = jnp.dot(merged.astype(BF16), wo_ref[...], preferred_element_type=F32)
    o_ref[0] = x_ref[0] + ada_ref[0, 2:3, :] * upd


def _out_proj(x, ada3, ya, yb, ma, mb, wa, wb, wo, tm):
    bsz, seq, d = x.shape
    tok = lambda width: pl.BlockSpec((1, tm, width), lambda b, i: (b, i, 0))
    return pl.pallas_call(
        _out_kernel,
        grid=(bsz, seq // tm),
        in_specs=[tok(d), pl.BlockSpec((1, 3, d), lambda b, i: (b, 0, 0)),
                  tok(WIDTH), tok(WIDTH), tok(d), tok(d),
                  _const_spec(wa.shape), _const_spec(wb.shape), _const_spec(wo.shape)],
        out_specs=tok(d),
        out_shape=jax.ShapeDtypeStruct((bsz, seq, d), x.dtype),
        compiler_params=pltpu.CompilerParams(
            dimension_semantics=("parallel", "arbitrary"),
            vmem_limit_bytes=VMEM_LIMIT),
        name="out_proj",
    )(x, ada3, ya, yb, ma, mb, wa, wb, wo)


def _tile_heads(vec, pad_to):
    v = jnp.pad(vec.astype(F32), (0, pad_to - vec.shape[0]))
    return jnp.tile(v, N_HEADS).reshape(1, N_HEADS * pad_to)


def _block_avg(sizes, total):
    idx = jnp.arange(total)
    gid = jnp.full((total,), -1, jnp.int32)
    inv = jnp.zeros((total,), F32)
    start = 0
    for n, (size, live) in enumerate(sizes):
        inside = (idx >= start) & (idx < start + size)
        if live:
            gid = jnp.where(inside, n, gid)
            inv = jnp.where(inside, 1.0 / size, inv)
        start += size
    same = (gid[:, None] == gid[None, :]) & (gid[:, None] >= 0)
    return jnp.where(same, inv[None, :], 0.0).astype(BF16)


def _shift_parts(g):
    hi = (-g).astype(BF16).astype(F32)
    mid = (-g - hi).astype(BF16).astype(F32)
    return hi, mid, -g - hi - mid


def kernel(x, c, positions, w_ada, b_ada, norm_w, w_in, b_f, q_lora_norm_w,
           kv_lora_norm_w, w_uq, w_ukv, qn_nope_a, qn_rope_a, kn_nope_a,
           kn_rope_a, qn_b, kn_b, w_branch_a, w_branch_b, w_out):
    bsz, seq, d = x.shape
    depth = w_in.shape[0]
    tm = 512
    tq = 512
    pad = HEAD_PAD - NOPE_DIM - ROPE_DIM

    inv_freq = ROPE_THETA ** (-jnp.arange(0, ROPE_DIM, 2, dtype=F32) / ROPE_DIM)
    invf = inv_freq.reshape(ROPE_DIM // 2, 1)
    pos3 = positions.reshape(bsz, 1, seq)
    head = 2 * [(NOPE_DIM, True), (ROPE_DIM, True), (pad, False)]
    g_a = _block_avg(head, MXU_DIM)
    g_b = _block_avg(4 * [(V_DIM, True)], MXU_DIM)
    sm_a = LOG2E / math.sqrt(NOPE_DIM + ROPE_DIM)
    sm_b = LOG2E / math.sqrt(V_DIM)

    for l in range(depth):
        ada3 = _ada_proj(c, w_ada[l], b_ada[l]).reshape(bsz, 3, d)

        wi = w_in[l]
        o = [0]
        for s in (Q_LORA, KV_LORA, ROPE_DIM, WIDTH, WIDTH, WIDTH, WIDTH, N_HEADS,
                  WIDTH, D_MODEL, D_MODEL):
            o.append(o[-1] + s)
        (w_cq, w_ckv, w_kr, w_ga, w_qb, w_kb, w_vb, w_f, w_gb, w_ma,
         w_mb) = [wi[:, o[n]:o[n + 1]] for n in range(11)]
        w_krf = jnp.concatenate(
            [w_f, jnp.zeros((d, ROPE_LANE0 - N_HEADS), F32), w_kr,
             jnp.zeros((d, pad), F32)], axis=1)
        win = jnp.concatenate([w_cq, w_ckv, w_krf, w_qb, w_kb, w_vb, w_ga, w_gb,
                               w_ma, w_mb], axis=1).astype(BF16)

        wuq = jnp.pad(w_uq[l].reshape(Q_LORA, N_HEADS, NOPE_DIM + ROPE_DIM),
                      ((0, 0), (0, 0), (0, pad))
                      ).reshape(Q_LORA, QK_WIDTH).astype(BF16)
        wkv = w_ukv[l].reshape(KV_LORA, N_HEADS, NOPE_DIM + V_DIM)
        wuk = jnp.pad(wkv[:, :, :NOPE_DIM], ((0, 0), (0, 0), (0, HEAD_PAD - NOPE_DIM))
                      ).reshape(KV_LORA, QK_WIDTH).astype(BF16)
        wuv = wkv[:, :, NOPE_DIM:].reshape(KV_LORA, WIDTH).astype(BF16)

        qaw = _tile_heads(jnp.concatenate([qn_nope_a[l], qn_rope_a[l]]) * sm_a, HEAD_PAD)
        kaw = _tile_heads(kn_nope_a[l], HEAD_PAD)
        krw = jnp.concatenate([jnp.zeros((ROPE_LANE0,), F32), kn_rope_a[l],
                               jnp.zeros((pad,), F32)]).reshape(1, LANES)
        qbw = _tile_heads(qn_b[l] * sm_b, V_DIM)
        kbw = _tile_heads(kn_b[l], V_DIM)
        bfv = jnp.pad(b_f[l].astype(F32), (0, LANES - N_HEADS)).reshape(1, LANES)

        amax = lambda v: jnp.max(jnp.abs(v.astype(F32)))
        bound_a = sm_a * BOUND_SLACK * jnp.sqrt(
            (NOPE_DIM * amax(qn_nope_a[l]) ** 2 + ROPE_DIM * amax(qn_rope_a[l]) ** 2)
            * (NOPE_DIM * amax(kn_nope_a[l]) ** 2 + ROPE_DIM * amax(kn_rope_a[l]) ** 2))
        bound_b = sm_b * BOUND_SLACK * V_DIM * amax(qn_b[l]) * amax(kn_b[l])
        bounded = jnp.maximum(bound_a, bound_b) <= SHIFT_MAX
        shift_a = jnp.where(bounded, bound_a, 0.0)
        shift_b = jnp.where(bounded, bound_b, 0.0)
        ga_shift = jnp.zeros((LANES,), F32).at[SHIFT_LANE0:SHIFT_LANE0 + BIAS_PARTS].set(
            jnp.stack(_shift_parts(shift_a))).reshape(1, LANES)
        gb_shift = jnp.full((1, LANES), shift_b, F32)
        flag = bounded.astype(jnp.int32).reshape(1)

        consts = [norm_w[l].reshape(1, d), win, wuq, wuk, wuv, g_a, g_b,
                  q_lora_norm_w[l].reshape(1, Q_LORA),
                  kv_lora_norm_w[l].reshape(1, KV_LORA), qaw, kaw, krw, qbw, kbw,
                  invf, bfv, ga_shift, gb_shift]
        (qa, ka, vat, qb, kb, vbt, ga, gb, ma, mb) = _in_proj(x, ada3, pos3, consts, tm)

        ya, yb = _attention(flag, qa, ka, vat, ga, qb, kb, vbt, gb, tq=tq)

        x = _out_proj(x, ada3, ya, yb, ma, mb, w_branch_a[l].astype(BF16),
                      w_branch_b[l].astype(BF16), w_out[l].astype(BF16), tm)
    return x
```

```python
import functools
import math

import jax
import jax.numpy as jnp
from jax import lax
from jax.experimental import pallas as pl
from jax.experimental.pallas import tpu as pltpu

D_MODEL = 1024
CHUNK = 64
N_HEADS = 8
NOPE_DIM = 64
ROPE_DIM = 32
V_DIM = 64
Q_LORA = 384
KV_LORA = 256
WIDTH = N_HEADS * V_DIM
ROPE_THETA = 10000.0
EPS = 1e-6
LOG2E = math.log2(math.e)

LANES = 128
HEAD_PAD = 128
QK_WIDTH = N_HEADS * HEAD_PAD
MXU_DIM = 256
BF16_ROWS = 16
VMEM_LIMIT = 48 * 1024 * 1024

C_CQ = 0
C_CKV = C_CQ + Q_LORA
C_KR = C_CKV + KV_LORA
C_QB = C_KR + LANES
C_KB = C_QB + WIDTH
C_VB = C_KB + WIDTH
C_GA = C_VB + WIDTH
C_GB = C_GA + WIDTH
C_MA = C_GB + WIDTH
C_MB = C_MA + D_MODEL
C_END = C_MB + D_MODEL
ROPE_LANE0 = NOPE_DIM
BIAS_LANE0 = V_DIM
BIAS_PARTS = 3
SHIFT_LANE0 = NOPE_DIM + ROPE_DIM
SHIFT_MAX = 48.0
BOUND_SLACK = 1.02

BF16 = jnp.bfloat16
F32 = jnp.float32


def _const_spec(shape):
    nd = len(shape)
    return pl.BlockSpec(shape, lambda *_: (0,) * nd, pipeline_mode=pl.Buffered(1))


def _ada_kernel(c_ref, w_ref, b_ref, o_ref):
    o_ref[...] = jnp.dot(c_ref[...].astype(BF16), w_ref[...].astype(BF16),
                         preferred_element_type=F32) + b_ref[...]


def _ada_proj(c, w_ada, b_ada):
    bsz, d = c.shape
    n = w_ada.shape[1]
    tn = 512
    return pl.pallas_call(
        _ada_kernel,
        grid=(n // tn,),
        in_specs=[pl.BlockSpec((bsz, d), lambda j: (0, 0)),
                  pl.BlockSpec((d, tn), lambda j: (0, j)),
                  pl.BlockSpec((1, tn), lambda j: (0, j))],
        out_specs=pl.BlockSpec((bsz, tn), lambda j: (0, j)),
        out_shape=jax.ShapeDtypeStruct((bsz, n), F32),
        name="ada_proj",
    )(c, w_ada, b_ada.reshape(1, n))


def _rms(x, n):
    return lax.rsqrt(jnp.sum(x * x, axis=-1, keepdims=True) * (1.0 / n) + EPS)


def _group_mean_sq(x, g_ref):
    g = g_ref[...]
    sq = (x * x).astype(BF16)
    parts = [jnp.dot(sq[:, c:c + MXU_DIM], g, preferred_element_type=F32)
             for c in range(0, x.shape[1], MXU_DIM)]
    return jnp.concatenate(parts, axis=1)


def _rot_half(x):
    lane = lax.broadcasted_iota(jnp.int32, x.shape, 1)
    half = ROPE_DIM // 2
    return jnp.where((lane & half) == 0,
                     pltpu.roll(x, LANES - half, axis=1),
                     pltpu.roll(x, half, axis=1))


def _head_tiles(x):
    tiles = []
    for hd in range(N_HEADS):
        blk = x[:, (hd // 2) * LANES:(hd // 2 + 1) * LANES]
        tiles.append(pltpu.roll(blk, V_DIM, axis=1) if hd % 2 else blk)
    return tiles


def _in_proj_kernel(x_ref, ada_ref, pos_ref, nw_ref, win_ref, wuqn_ref, wuqr_ref,
                    wuk_ref, wuv_ref, gb_mat_ref, gr_mat_ref, qlw_ref, kvlw_ref,
                    qanw_ref, qarw_ref, kaw_ref, krw_ref, qbw_ref, kbw_ref, invf_ref,
                    bf_ref, ga_shift_ref, gb_shift_ref,
                    qa_ref, ka_ref, vat_ref, qb_ref, kb_ref, vbt_ref,
                    ga_ref, gb_ref, ma_ref, mb_ref, carry_ref):
    tm = x_ref.shape[1]
    x = x_ref[0]
    shift = ada_ref[0, 0:1, :]
    scale = ada_ref[0, 1:2, :]
    h = (x * _rms(x, D_MODEL)) * (nw_ref[...] * (1.0 + scale)) + shift
    hb = h.astype(BF16)

    def proj(lo, hi):
        return jnp.dot(hb, win_ref[:, lo:hi], preferred_element_type=F32)

    pieces = [(ga_ref, C_GA, C_GB), (gb_ref, C_GB, C_MA), (ma_ref, C_MA, C_MB),
              (mb_ref, C_MB, C_END)]

    def fill(n=1):
        for _ in range(min(n, len(pieces))):
            ref, lo, hi = pieces.pop(0)
            ref[0] = proj(lo, hi).astype(BF16)

    ang = invf_ref[...] * pos_ref[0].astype(F32)
    cos_h = jnp.cos(ang)
    sin_h = jnp.sin(ang)
    groups = LANES // ROPE_DIM
    cos_t = jnp.concatenate(groups * [cos_h, cos_h], axis=0).T
    sin_t = jnp.concatenate(groups * [-sin_h, sin_h], axis=0).T
    lane = lax.broadcasted_iota(jnp.int32, (tm, LANES), 1)

    def rope(t):
        return t * cos_t + _rot_half(t) * sin_t

    lat = proj(C_CQ, C_QB)
    cq = lat[:, C_CQ:C_CKV]
    ckv = lat[:, C_CKV:C_KR]
    kr = lat[:, C_KR:C_QB]
    fill(2)

    cqn = (cq * _rms(cq, Q_LORA) * qlw_ref[...]).astype(BF16)
    qnp = jnp.dot(cqn, wuqn_ref[...], preferred_element_type=F32)
    qrp = jnp.dot(cqn, wuqr_ref[...], preferred_element_type=F32)
    qnp = qnp * lax.rsqrt(_group_mean_sq(qnp, gb_mat_ref) + EPS) * qanw_ref[...]
    qrp = qrp * lax.rsqrt(_group_mean_sq(qrp, gr_mat_ref) + EPS) * qarw_ref[...]
    qrp = [rope(qrp[:, c:c + LANES]) for c in range(0, qrp.shape[1], LANES)]
    head_lane = lane < V_DIM
    rope_lane = (lane >= ROPE_LANE0) & (lane < ROPE_LANE0 + ROPE_DIM)
    shift_lane = (lane >= SHIFT_LANE0) & (lane < SHIFT_LANE0 + BIAS_PARTS)
    ones_a = jnp.where(shift_lane, 1.0, 0.0)
    for hd, t in enumerate(_head_tiles(qnp)):
        grp = hd % groups
        rp = qrp[hd // groups]
        move = (ROPE_LANE0 - grp * ROPE_DIM) % LANES
        rp = pltpu.roll(rp, move, axis=1) if move else rp
        qa_ref[0, :, hd * HEAD_PAD:(hd + 1) * HEAD_PAD] = jnp.where(
            head_lane, t, jnp.where(rope_lane, rp, ones_a)).astype(BF16)
    fill()

    ckvn = (ckv * _rms(ckv, KV_LORA) * kvlw_ref[...]).astype(BF16)
    knp = jnp.dot(ckvn, wuk_ref[...], preferred_element_type=F32)
    knp = knp * lax.rsqrt(_group_mean_sq(knp, gb_mat_ref) + EPS) * kaw_ref[...]
    kr_ms = jnp.sum(jnp.where(rope_lane, kr * kr, 0.0), axis=-1, keepdims=True)
    krn = kr * lax.rsqrt(kr_ms * (1.0 / ROPE_DIM) + EPS) * krw_ref[...]
    krr = rope(krn) + ga_shift_ref[...]
    for hd, t in enumerate(_head_tiles(knp)):
        ka_ref[0, :, hd * HEAD_PAD:(hd + 1) * HEAD_PAD] = jnp.where(
            head_lane, t, krr).astype(BF16)
    va = jnp.dot(ckvn, wuv_ref[...], preferred_element_type=F32)
    vat_ref[0] = va.T.astype(BF16)
    fill()

    z = kr + bf_ref[...]
    logf = jnp.minimum(z, 0.0) - jnp.log1p(jnp.exp(-jnp.abs(z)))
    c = jnp.where(lane < N_HEADS, logf, 0.0)
    row = lax.broadcasted_iota(jnp.int32, c.shape, 0)
    step = 1
    while step < tm:
        c = c + jnp.where(row >= step, pltpu.roll(c, step, axis=0), 0.0)
        step *= 2

    @pl.when(pl.program_id(1) == 0)
    def _():
        carry_ref[...] = jnp.zeros_like(carry_ref)

    c = c + carry_ref[...]
    carry_ref[...] = c[tm - 1:tm, :]
    c2 = c * LOG2E

    def parts(v):
        hi = v.astype(BF16).astype(F32)
        mid = (v - hi).astype(BF16).astype(F32)
        return hi, mid, v - hi - mid

    shared = jnp.zeros_like(c2)
    for n, part in enumerate(parts(-c2 - gb_shift_ref[...]) + parts(c2)):
        lo_lane = BIAS_LANE0 + n * N_HEADS
        shared = jnp.where((lane >= lo_lane) & (lane < lo_lane + N_HEADS),
                           pltpu.roll(part, lo_lane, axis=1), shared)
    q_part_lane0 = BIAS_LANE0 + BIAS_PARTS * N_HEADS
    in_k_parts = (lane >= BIAS_LANE0) & (lane < q_part_lane0)
    in_q_parts = (lane >= q_part_lane0) & (lane < q_part_lane0 + BIAS_PARTS * N_HEADS)
    k_tail = jnp.where(in_k_parts, shared, 0.0)
    q_tail = jnp.where(in_q_parts, shared, 0.0)
    lane_row = lane[:1]
    own = [(lane_row & (N_HEADS - 1)) == hd for hd in range(N_HEADS)]
    ones_q = [jnp.where(in_k_parts[:1] & o, 1.0, 0.0) for o in own]
    ones_k = [jnp.where(in_q_parts[:1] & o, 1.0, 0.0) for o in own]

    qb = proj(C_QB, C_KB)
    qbn = qb * lax.rsqrt(_group_mean_sq(qb, gb_mat_ref) + EPS) * qbw_ref[...]
    for hd, t in enumerate(_head_tiles(qbn)):
        qb_ref[0, :, hd * HEAD_PAD:(hd + 1) * HEAD_PAD] = jnp.where(
            head_lane, t, q_tail + ones_q[hd]).astype(BF16)
    kb = proj(C_KB, C_VB)
    kbn = kb * lax.rsqrt(_group_mean_sq(kb, gb_mat_ref) + EPS) * kbw_ref[...]
    for hd, t in enumerate(_head_tiles(kbn)):
        kb_ref[0, :, hd * HEAD_PAD:(hd + 1) * HEAD_PAD] = jnp.where(
            head_lane, t, k_tail + ones_k[hd]).astype(BF16)
    vbt_ref[0] = proj(C_VB, C_GA).T.astype(BF16)
    assert not pieces


def _in_proj(x, ada3, pos3, consts, tm):
    bsz, seq, d = x.shape
    grid = (bsz, seq // tm)
    tok = lambda width: pl.BlockSpec((1, tm, width), lambda b, i: (b, i, 0))
    tok_t = pl.BlockSpec((1, WIDTH, tm), lambda b, i: (b, 0, i))
    in_specs = [tok(d),
                pl.BlockSpec((1, 3, d), lambda b, i: (b, 0, 0)),
                pl.BlockSpec((1, 1, tm), lambda b, i: (b, 0, i))]
    in_specs += [_const_spec(a.shape) for a in consts]
    row_major = lambda w: jax.ShapeDtypeStruct((bsz, seq, w), BF16)
    transposed = jax.ShapeDtypeStruct((bsz, WIDTH, seq), BF16)
    out_shape = [row_major(QK_WIDTH), row_major(QK_WIDTH), transposed,
                 row_major(QK_WIDTH), row_major(QK_WIDTH), transposed,
                 row_major(WIDTH), row_major(WIDTH), row_major(d), row_major(d)]
    out_specs = [tok(QK_WIDTH), tok(QK_WIDTH), tok_t,
                 tok(QK_WIDTH), tok(QK_WIDTH), tok_t,
                 tok(WIDTH), tok(WIDTH), tok(d), tok(d)]
    return pl.pallas_call(
        _in_proj_kernel,
        grid=grid,
        in_specs=in_specs,
        out_specs=out_specs,
        out_shape=out_shape,
        scratch_shapes=[pltpu.VMEM((1, LANES), F32)],
        compiler_params=pltpu.CompilerParams(
            dimension_semantics=("parallel", "arbitrary"),
            vmem_limit_bytes=VMEM_LIMIT),
        name="in_proj",
    )(x, ada3, pos3, *consts)


def _attn_kernel(flag_ref, qa_ref, ka_ref, vat_ref, ga_ref, qb_ref, kb_ref, vbt_ref,
                 gb_ref, oa_ref, ob_ref, *, tq, n_q_tiles):
    half = tq // 2
    nt = (((1,), (1,)), ((), ()))
    key = lax.broadcasted_iota(jnp.int32, (half, half), 0)
    qry = lax.broadcasted_iota(jnp.int32, (half, half), 1)
    ones = jnp.ones((BF16_ROWS, n_q_tiles * tq), BF16)
    branches = [(qa_ref, ka_ref, vat_ref, ga_ref, oa_ref, (key // CHUNK) <= (qry // CHUNK)),
                (qb_ref, kb_ref, vbt_ref, gb_ref, ob_ref, key <= qry)]
    chains = [(br, hh) for br in range(2) for hh in range(2)]

    def lanes(hh):
        return slice(hh * HEAD_PAD, (hh + 1) * HEAD_PAD)

    def values(br, hh, lo, hi):
        vt = branches[br][2][0, hh * V_DIM:(hh + 1) * V_DIM, lo:hi]
        return jnp.concatenate([vt, ones[:, :hi - lo]], axis=0)

    def probs(st, m, in_f32):
        if in_f32:
            return jnp.exp2(st).astype(BF16)
        return jnp.exp2((st - m).astype(BF16))

    def full_scores(br, hh, t, j):
        q_ref, k_ref = branches[br][:2]
        st = lax.dot_general(k_ref[0, j * tq:(j + 1) * tq, lanes(hh)],
                             q_ref[0, t * tq:(t + 1) * tq, lanes(hh)], nt,
                             preferred_element_type=F32)
        return st, jnp.max(st, axis=0, keepdims=True)

    def full_pv(br, hh, j, st, m, in_f32=False):
        return jnp.dot(values(br, hh, j * tq, (j + 1) * tq), probs(st, m, in_f32),
                       preferred_element_type=F32)

    def diag_scores(br, hh, t):
        q_ref, k_ref = branches[br][:2]
        mask = branches[br][5]
        base = t * tq
        st_lo = lax.dot_general(k_ref[0, base:base + half, lanes(hh)],
                                q_ref[0, base:base + tq, lanes(hh)], nt,
                                preferred_element_type=F32)
        st_hi = lax.dot_general(k_ref[0, base + half:base + tq, lanes(hh)],
                                q_ref[0, base + half:base + tq, lanes(hh)], nt,
                                preferred_element_type=F32)
        st_lo = jnp.concatenate(
            [jnp.where(mask, st_lo[:, :half], -jnp.inf), st_lo[:, half:]], axis=1)
        st_hi = jnp.where(mask, st_hi, -jnp.inf)
        mx_lo = jnp.max(st_lo, axis=0, keepdims=True)
        mx_hi = jnp.max(st_hi, axis=0, keepdims=True)
        mx = jnp.concatenate(
            [mx_lo[:, :half], jnp.maximum(mx_lo[:, half:], mx_hi)], axis=1)
        return (st_lo, st_hi), mx

    def diag_pv(br, hh, t, st, m, in_f32=False):
        st_lo, st_hi = st
        base = t * tq
        pv = jnp.dot(values(br, hh, base, base + half), probs(st_lo, m, in_f32),
                     preferred_element_type=F32)
        pv_hi = jnp.dot(values(br, hh, base + half, base + tq),
                        probs(st_hi, m[:, half:], in_f32), preferred_element_type=F32)
        return jnp.concatenate([pv[:, :half], pv[:, half:] + pv_hi], axis=1)

    def scores(br, hh, t, j):
        return diag_scores(br, hh, t) if j == t else full_scores(br, hh, t, j)

    def update(br, hh, t, j, st, mx, state):
        m_new = mx if state is None else jnp.maximum(state[0], mx)
        pv = diag_pv(br, hh, t, st, m_new) if j == t else full_pv(br, hh, j, st, m_new)
        if state is None:
            return m_new, pv
        m, acc = state
        return m_new, jnp.exp2(m - m_new) * acc + pv

    def finish(t, accs):
        for br in range(2):
            g_ref, o_ref = branches[br][3:5]
            halves = [accs[(br, hh)] for hh in range(2)]
            halves = [acc[:V_DIM] / acc[V_DIM:V_DIM + 1] for acc in halves]
            o = jnp.concatenate(halves, axis=0).T
            g = g_ref[0, t * tq:(t + 1) * tq, :].astype(F32)
            o_ref[0, t * tq:(t + 1) * tq, :] = (o * (g * jax.nn.sigmoid(g))).astype(BF16)

    def shifted_path():
        zero = jnp.zeros((1, tq), F32)

        def past_scores(br, hh, t):
            q_ref, k_ref = branches[br][:2]
            return lax.dot_general(k_ref[0, :t * tq, lanes(hh)],
                                   q_ref[0, t * tq:(t + 1) * tq, lanes(hh)], nt,
                                   preferred_element_type=F32)

        for t in range(n_q_tiles):
            diag = {c: diag_scores(*c, t)[0] for c in chains}
            past = {c: past_scores(*c, t) for c in chains} if t else None
            accs = {}
            for c in chains:
                acc = diag_pv(*c, t, diag[c], zero, in_f32=True)
                if t:
                    acc = acc + jnp.dot(values(*c, 0, t * tq), probs(past[c], zero, True),
                                        preferred_element_type=F32)
                accs[c] = acc
            finish(t, accs)

    def online_path():
        for t in range(n_q_tiles):
            state = {c: None for c in chains}
            nxt = {c: scores(*c, t, 0) for c in chains}
            for j in range(t + 1):
                cur = nxt
                if j < t:
                    nxt = {c: scores(*c, t, j + 1) for c in chains}
                state = {c: update(*c, t, j, *cur[c], state[c]) for c in chains}
            finish(t, {c: state[c][1] for c in chains})

    pl.when(flag_ref[0] != 0)(shifted_path)
    pl.when(flag_ref[0] == 0)(online_path)


def _attention(flag, qa, ka, vat, ga, qb, kb, vbt, gb, *, tq):
    bsz, seq, _ = qa.shape
    pairs = N_HEADS // 2
    kern = functools.partial(_attn_kernel, tq=tq, n_q_tiles=seq // tq)
    qk_spec = pl.BlockSpec((1, seq, 2 * HEAD_PAD), lambda b, p: (b, 0, p))
    vt_spec = pl.BlockSpec((1, 2 * V_DIM, seq), lambda b, p: (b, p, 0))
    tok_spec = pl.BlockSpec((1, seq, 2 * V_DIM), lambda b, p: (b, 0, p))
    out = jax.ShapeDtypeStruct((bsz, seq, WIDTH), BF16)
    return pl.pallas_call(
        kern,
        grid=(bsz, pairs),
        in_specs=[pl.BlockSpec(memory_space=pltpu.SMEM)]
        + 2 * [qk_spec, qk_spec, vt_spec, tok_spec],
        out_specs=[tok_spec, tok_spec],
        out_shape=[out, out],
        compiler_params=pltpu.CompilerParams(
            dimension_semantics=("parallel", "arbitrary"),
            vmem_limit_bytes=VMEM_LIMIT),
        name="attn",
    )(flag, qa, ka, vat, ga, qb, kb, vbt, gb)


def _out_kernel(x_ref, ada_ref, ya_ref, yb_ref, ma_ref, mb_ref, wa_ref, wb_ref,
                wo_ref, o_ref):
    pa = jnp.dot(ya_ref[0], wa_ref[...], preferred_element_type=F32)
    pb = jnp.dot(yb_ref[0], wb_ref[...], preferred_element_type=F32)
    merged = (jax.nn.sigmoid(ma_ref[0].astype(F32)) * pa
              + jax.nn.sigmoid(mb_ref[0].astype(F32)) * pb)
    upd = jnp.dot(merged.astype(BF16), wo_ref[...], preferred_element_type=F32)
    o_ref[0] = x_ref[0] + ada_ref[0, 2:3, :] * upd


def _out_proj(x, ada3, ya, yb, ma, mb, wa, wb, wo, tm):
    bsz, seq, d = x.shape
    tok = lambda width: pl.BlockSpec((1, tm, width), lambda b, i: (b, i, 0))
    return pl.pallas_call(
        _out_kernel,
        grid=(bsz, seq // tm),
        in_specs=[tok(d), pl.BlockSpec((1, 3, d), lambda b, i: (b, 0, 0)),
                  tok(WIDTH), tok(WIDTH), tok(d), tok(d),
                  _const_spec(wa.shape), _const_spec(wb.shape), _const_spec(wo.shape)],
        out_specs=tok(d),
        out_shape=jax.ShapeDtypeStruct((bsz, seq, d), x.dtype),
        compiler_params=pltpu.CompilerParams(
            dimension_semantics=("parallel", "arbitrary"),
            vmem_limit_bytes=VMEM_LIMIT),
        name="out_proj",
    )(x, ada3, ya, yb, ma, mb, wa, wb, wo)


def _tile_heads(vec, pad_to):
    v = jnp.pad(vec.astype(F32), (0, pad_to - vec.shape[0]))
    return jnp.tile(v, N_HEADS).reshape(1, N_HEADS * pad_to)


def _block_avg(size, total):
    gid = jnp.arange(total) // size
    return jnp.where(gid[:, None] == gid[None, :], 1.0 / size, 0.0).astype(BF16)


def _shift_parts(g):
    hi = (-g).astype(BF16).astype(F32)
    mid = (-g - hi).astype(BF16).astype(F32)
    return hi, mid, -g - hi - mid


def kernel(x, c, positions, w_ada, b_ada, norm_w, w_in, b_f, q_lora_norm_w,
           kv_lora_norm_w, w_uq, w_ukv, qn_nope_a, qn_rope_a, kn_nope_a,
           kn_rope_a, qn_b, kn_b, w_branch_a, w_branch_b, w_out):
    bsz, seq, d = x.shape
    depth = w_in.shape[0]
    tm = 512
    tq = 512
    pad = HEAD_PAD - NOPE_DIM - ROPE_DIM

    inv_freq = ROPE_THETA ** (-jnp.arange(0, ROPE_DIM, 2, dtype=F32) / ROPE_DIM)
    invf = inv_freq.reshape(ROPE_DIM // 2, 1)
    pos3 = positions.reshape(bsz, 1, seq)
    g_b = _block_avg(V_DIM, MXU_DIM)
    g_r = _block_avg(ROPE_DIM, MXU_DIM)
    sm_a = LOG2E / math.sqrt(NOPE_DIM + ROPE_DIM)
    sm_b = LOG2E / math.sqrt(V_DIM)

    for l in range(depth):
        ada3 = _ada_proj(c, w_ada[l], b_ada[l]).reshape(bsz, 3, d)

        wi = w_in[l]
        o = [0]
        for s in (Q_LORA, KV_LORA, ROPE_DIM, WIDTH, WIDTH, WIDTH, WIDTH, N_HEADS,
                  WIDTH, D_MODEL, D_MODEL):
            o.append(o[-1] + s)
        (w_cq, w_ckv, w_kr, w_ga, w_qb, w_kb, w_vb, w_f, w_gb, w_ma,
         w_mb) = [wi[:, o[n]:o[n + 1]] for n in range(11)]
        w_krf = jnp.concatenate(
            [w_f, jnp.zeros((d, ROPE_LANE0 - N_HEADS), F32), w_kr,
             jnp.zeros((d, pad), F32)], axis=1)
        win = jnp.concatenate([w_cq, w_ckv, w_krf, w_qb, w_kb, w_vb, w_ga, w_gb,
                               w_ma, w_mb], axis=1).astype(BF16)

        wq = w_uq[l].reshape(Q_LORA, N_HEADS, NOPE_DIM + ROPE_DIM)
        wuqn = wq[:, :, :NOPE_DIM].reshape(Q_LORA, N_HEADS * NOPE_DIM).astype(BF16)
        wuqr = wq[:, :, NOPE_DIM:].reshape(Q_LORA, N_HEADS * ROPE_DIM).astype(BF16)
        wkv = w_ukv[l].reshape(KV_LORA, N_HEADS, NOPE_DIM + V_DIM)
        wuk = wkv[:, :, :NOPE_DIM].reshape(KV_LORA, N_HEADS * NOPE_DIM).astype(BF16)
        wuv = wkv[:, :, NOPE_DIM:].reshape(KV_LORA, WIDTH).astype(BF16)

        qanw = _tile_heads(qn_nope_a[l] * sm_a, NOPE_DIM)
        qarw = _tile_heads(qn_rope_a[l] * sm_a, ROPE_DIM)
        kaw = _tile_heads(kn_nope_a[l], NOPE_DIM)
        krw = jnp.concatenate([jnp.zeros((ROPE_LANE0,), F32), kn_rope_a[l],
                               jnp.zeros((pad,), F32)]).reshape(1, LANES)
        qbw = _tile_heads(qn_b[l] * sm_b, V_DIM)
        kbw = _tile_heads(kn_b[l], V_DIM)
        bfv = jnp.pad(b_f[l].astype(F32), (0, LANES - N_HEADS)).reshape(1, LANES)

        amax = lambda v: jnp.max(jnp.abs(v.astype(F32)))
        bound_a = sm_a * BOUND_SLACK * jnp.sqrt(
            (NOPE_DIM * amax(qn_nope_a[l]) ** 2 + ROPE_DIM * amax(qn_rope_a[l]) ** 2)
            * (NOPE_DIM * amax(kn_nope_a[l]) ** 2 + ROPE_DIM * amax(kn_rope_a[l]) ** 2))
        bound_b = sm_b * BOUND_SLACK * V_DIM * amax(qn_b[l]) * amax(kn_b[l])
        bounded = jnp.maximum(bound_a, bound_b) <= SHIFT_MAX
        shift_a = jnp.where(bounded, bound_a, 0.0)
        shift_b = jnp.where(bounded, bound_b, 0.0)
        ga_shift = jnp.zeros((LANES,), F32).at[SHIFT_LANE0:SHIFT_LANE0 + BIAS_PARTS].set(
            jnp.stack(_shift_parts(shift_a))).reshape(1, LANES)
        gb_shift = jnp.full((1, LANES), shift_b, F32)
        flag = bounded.astype(jnp.int32).reshape(1)

        consts = [norm_w[l].reshape(1, d), win, wuqn, wuqr, wuk, wuv, g_b, g_r,
                  q_lora_norm_w[l].reshape(1, Q_LORA),
                  kv_lora_norm_w[l].reshape(1, KV_LORA), qanw, qarw, kaw, krw, qbw,
                  kbw, invf, bfv, ga_shift, gb_shift]
        (qa, ka, vat, qb, kb, vbt, ga, gb, ma, mb) = _in_proj(x, ada3, pos3, consts, tm)

        ya, yb = _attention(flag, qa, ka, vat, ga, qb, kb, vbt, gb, tq=tq)

        x = _out_proj(x, ada3, ya, yb, ma, mb, w_branch_a[l].astype(BF16),
                      w_branch_b[l].astype(BF16), w_out[l].astype(BF16), tm)
    return x
```

```python
import functools
import math

import jax
import jax.numpy as jnp
from jax import lax
from jax.experimental import pallas as pl
from jax.experimental.pallas import tpu as pltpu

D_MODEL = 1024
CHUNK = 64
N_HEADS = 8
NOPE_DIM = 64
ROPE_DIM = 32
V_DIM = 64
Q_LORA = 384
KV_LORA = 256
WIDTH = N_HEADS * V_DIM
ROPE_THETA = 10000.0
EPS = 1e-6
LOG2E = math.log2(math.e)

LANES = 128
HEAD_PAD = 128
QK_WIDTH = N_HEADS * HEAD_PAD
MXU_DIM = 256
BF16_ROWS = 16
VMEM_LIMIT = 48 * 1024 * 1024
OUT_ROWS = 512

C_CQ = 0
C_CKV = C_CQ + Q_LORA
C_KR = C_CKV + KV_LORA
C_QB = C_KR + LANES
C_KB = C_QB + WIDTH
C_VB = C_KB + WIDTH
C_GA = C_VB + WIDTH
C_GB = C_GA + WIDTH
C_MA = C_GB + WIDTH
C_MB = C_MA + D_MODEL
C_END = C_MB + D_MODEL
ROPE_LANE0 = NOPE_DIM
BIAS_LANE0 = V_DIM
BIAS_PARTS = 3
SHIFT_LANE0 = NOPE_DIM + ROPE_DIM
SHIFT_MAX = 48.0
BOUND_SLACK = 1.02

BF16 = jnp.bfloat16
F32 = jnp.float32


def _const_spec(shape):
    nd = len(shape)
    return pl.BlockSpec(shape, lambda *_: (0,) * nd, pipeline_mode=pl.Buffered(1))


def _ada_kernel(c_ref, w_ref, b_ref, o_ref):
    o_ref[...] = jnp.dot(c_ref[...].astype(BF16), w_ref[...].astype(BF16),
                         preferred_element_type=F32) + b_ref[...]


def _ada_proj(c, w_ada, b_ada):
    bsz, d = c.shape
    n = w_ada.shape[1]
    tn = 512
    return pl.pallas_call(
        _ada_kernel,
        grid=(n // tn,),
        in_specs=[pl.BlockSpec((bsz, d), lambda j: (0, 0)),
                  pl.BlockSpec((d, tn), lambda j: (0, j)),
                  pl.BlockSpec((1, tn), lambda j: (0, j))],
        out_specs=pl.BlockSpec((bsz, tn), lambda j: (0, j)),
        out_shape=jax.ShapeDtypeStruct((bsz, n), F32),
        name="ada_proj",
    )(c, w_ada, b_ada.reshape(1, n))


def _rms(x, n):
    return lax.rsqrt(jnp.sum(x * x, axis=-1, keepdims=True) * (1.0 / n) + EPS)


def _group_mean_sq(x, g_ref):
    g = g_ref[...]
    sq = (x * x).astype(BF16)
    parts = [jnp.dot(sq[:, c:c + MXU_DIM], g, preferred_element_type=F32)
             for c in range(0, x.shape[1], MXU_DIM)]
    return jnp.concatenate(parts, axis=1)


def _rot_half(x):
    lane = lax.broadcasted_iota(jnp.int32, x.shape, 1)
    half = ROPE_DIM // 2
    return jnp.where((lane & half) == 0,
                     pltpu.roll(x, LANES - half, axis=1),
                     pltpu.roll(x, half, axis=1))


def _head_tiles(x):
    tiles = []
    for hd in range(N_HEADS):
        blk = x[:, (hd // 2) * LANES:(hd // 2 + 1) * LANES]
        tiles.append(pltpu.roll(blk, V_DIM, axis=1) if hd % 2 else blk)
    return tiles


def _in_proj_kernel(x_ref, ada_ref, pos_ref, nw_ref, win_ref, wuqn_ref, wuqr_ref,
                    wuk_ref, wuv_ref, gb_mat_ref, gr_mat_ref, qlw_ref, kvlw_ref,
                    qanw_ref, qarw_ref, kaw_ref, krw_ref, qbw_ref, kbw_ref, invf_ref,
                    bf_ref, ga_shift_ref, gb_shift_ref,
                    qa_ref, ka_ref, vat_ref, qb_ref, kb_ref, vbt_ref,
                    ga_ref, gb_ref, ma_ref, mb_ref, carry_ref):
    tm = x_ref.shape[1]
    x = x_ref[0]
    shift = ada_ref[0, 0:1, :]
    scale = ada_ref[0, 1:2, :]
    h = (x * _rms(x, D_MODEL)) * (nw_ref[...] * (1.0 + scale)) + shift
    hb = h.astype(BF16)

    def proj(lo, hi):
        return jnp.dot(hb, win_ref[:, lo:hi], preferred_element_type=F32)

    pieces = [(ga_ref, C_GA, C_GB), (gb_ref, C_GB, C_MA), (ma_ref, C_MA, C_MB),
              (mb_ref, C_MB, C_END)]

    def fill(n=1):
        for _ in range(min(n, len(pieces))):
            ref, lo, hi = pieces.pop(0)
            ref[0] = proj(lo, hi).astype(BF16)

    ang = invf_ref[...] * pos_ref[0].astype(F32)
    cos_h = jnp.cos(ang)
    sin_h = jnp.sin(ang)
    groups = LANES // ROPE_DIM
    cos_t = jnp.concatenate(groups * [cos_h, cos_h], axis=0).T
    sin_t = jnp.concatenate(groups * [-sin_h, sin_h], axis=0).T
    lane = lax.broadcasted_iota(jnp.int32, (tm, LANES), 1)

    def rope(t):
        return t * cos_t + _rot_half(t) * sin_t

    lat = proj(C_CQ, C_QB)
    cq = lat[:, C_CQ:C_CKV]
    ckv = lat[:, C_CKV:C_KR]
    kr = lat[:, C_KR:C_QB]
    fill(2)

    cqn = (cq * _rms(cq, Q_LORA) * qlw_ref[...]).astype(BF16)
    qnp = jnp.dot(cqn, wuqn_ref[...], preferred_element_type=F32)
    qrp = jnp.dot(cqn, wuqr_ref[...], preferred_element_type=F32)
    qnp = qnp * lax.rsqrt(_group_mean_sq(qnp, gb_mat_ref) + EPS) * qanw_ref[...]
    qrp = qrp * lax.rsqrt(_group_mean_sq(qrp, gr_mat_ref) + EPS) * qarw_ref[...]
    qrp = [rope(qrp[:, c:c + LANES]) for c in range(0, qrp.shape[1], LANES)]
    head_lane = lane < V_DIM
    rope_lane = (lane >= ROPE_LANE0) & (lane < ROPE_LANE0 + ROPE_DIM)
    shift_lane = (lane >= SHIFT_LANE0) & (lane < SHIFT_LANE0 + BIAS_PARTS)
    ones_a = jnp.where(shift_lane, 1.0, 0.0)
    for hd, t in enumerate(_head_tiles(qnp)):
        grp = hd % groups
        rp = qrp[hd // groups]
        move = (ROPE_LANE0 - grp * ROPE_DIM) % LANES
        rp = pltpu.roll(rp, move, axis=1) if move else rp
        qa_ref[0, :, hd * HEAD_PAD:(hd + 1) * HEAD_PAD] = jnp.where(
            head_lane, t, jnp.where(rope_lane, rp, ones_a)).astype(BF16)
    fill()

    ckvn = (ckv * _rms(ckv, KV_LORA) * kvlw_ref[...]).astype(BF16)
    knp = jnp.dot(ckvn, wuk_ref[...], preferred_element_type=F32)
    knp = knp * lax.rsqrt(_group_mean_sq(knp, gb_mat_ref) + EPS) * kaw_ref[...]
    kr_ms = jnp.sum(jnp.where(rope_lane, kr * kr, 0.0), axis=-1, keepdims=True)
    krn = kr * lax.rsqrt(kr_ms * (1.0 / ROPE_DIM) + EPS) * krw_ref[...]
    krr = rope(krn) + ga_shift_ref[...]
    for hd, t in enumerate(_head_tiles(knp)):
        ka_ref[0, :, hd * HEAD_PAD:(hd + 1) * HEAD_PAD] = jnp.where(
            head_lane, t, krr).astype(BF16)
    va = jnp.dot(ckvn, wuv_ref[...], preferred_element_type=F32)
    vat_ref[0] = va.T.astype(BF16)
    fill()

    z = kr + bf_ref[...]
    logf = jnp.minimum(z, 0.0) - jnp.log1p(jnp.exp(-jnp.abs(z)))
    c = jnp.where(lane < N_HEADS, logf, 0.0)
    row = lax.broadcasted_iota(jnp.int32, c.shape, 0)
    step = 1
    while step < tm:
        c = c + jnp.where(row >= step, pltpu.roll(c, step, axis=0), 0.0)
        step *= 2

    @pl.when(pl.program_id(1) == 0)
    def _():
        carry_ref[...] = jnp.zeros_like(carry_ref)

    c = c + carry_ref[...]
    carry_ref[...] = c[tm - 1:tm, :]
    c2 = c * LOG2E

    def parts(v):
        hi = v.astype(BF16).astype(F32)
        mid = (v - hi).astype(BF16).astype(F32)
        return hi, mid, v - hi - mid

    shared = jnp.zeros_like(c2)
    for n, part in enumerate(parts(-c2 - gb_shift_ref[...]) + parts(c2)):
        lo_lane = BIAS_LANE0 + n * N_HEADS
        shared = jnp.where((lane >= lo_lane) & (lane < lo_lane + N_HEADS),
                           pltpu.roll(part, lo_lane, axis=1), shared)
    q_part_lane0 = BIAS_LANE0 + BIAS_PARTS * N_HEADS
    in_k_parts = (lane >= BIAS_LANE0) & (lane < q_part_lane0)
    in_q_parts = (lane >= q_part_lane0) & (lane < q_part_lane0 + BIAS_PARTS * N_HEADS)
    k_tail = jnp.where(in_k_parts, shared, 0.0)
    q_tail = jnp.where(in_q_parts, shared, 0.0)
    lane_row = lane[:1]
    own = [(lane_row & (N_HEADS - 1)) == hd for hd in range(N_HEADS)]
    ones_q = [jnp.where(in_k_parts[:1] & o, 1.0, 0.0) for o in own]
    ones_k = [jnp.where(in_q_parts[:1] & o, 1.0, 0.0) for o in own]

    qb = proj(C_QB, C_KB)
    qbn = qb * lax.rsqrt(_group_mean_sq(qb, gb_mat_ref) + EPS) * qbw_ref[...]
    for hd, t in enumerate(_head_tiles(qbn)):
        qb_ref[0, :, hd * HEAD_PAD:(hd + 1) * HEAD_PAD] = jnp.where(
            head_lane, t, q_tail + ones_q[hd]).astype(BF16)
    kb = proj(C_KB, C_VB)
    kbn = kb * lax.rsqrt(_group_mean_sq(kb, gb_mat_ref) + EPS) * kbw_ref[...]
    for hd, t in enumerate(_head_tiles(kbn)):
        kb_ref[0, :, hd * HEAD_PAD:(hd + 1) * HEAD_PAD] = jnp.where(
            head_lane, t, k_tail + ones_k[hd]).astype(BF16)
    vbt_ref[0] = proj(C_VB, C_GA).T.astype(BF16)
    assert not pieces


def _in_proj(x, ada3, pos3, consts, tm):
    bsz, seq, d = x.shape
    grid = (bsz, seq // tm)
    tok = lambda width: pl.BlockSpec((1, tm, width), lambda b, i: (b, i, 0))
    tok_t = pl.BlockSpec((1, WIDTH, tm), lambda b, i: (b, 0, i))
    in_specs = [tok(d),
                pl.BlockSpec((1, 3, d), lambda b, i: (b, 0, 0)),
                pl.BlockSpec((1, 1, tm), lambda b, i: (b, 0, i))]
    in_specs += [_const_spec(a.shape) for a in consts]
    row_major = lambda w: jax.ShapeDtypeStruct((bsz, seq, w), BF16)
    transposed = jax.ShapeDtypeStruct((bsz, WIDTH, seq), BF16)
    out_shape = [row_major(QK_WIDTH), row_major(QK_WIDTH), transposed,
                 row_major(QK_WIDTH), row_major(QK_WIDTH), transposed,
                 row_major(WIDTH), row_major(WIDTH), row_major(d), row_major(d)]
    out_specs = [tok(QK_WIDTH), tok(QK_WIDTH), tok_t,
                 tok(QK_WIDTH), tok(QK_WIDTH), tok_t,
                 tok(WIDTH), tok(WIDTH), tok(d), tok(d)]
    return pl.pallas_call(
        _in_proj_kernel,
        grid=grid,
        in_specs=in_specs,
        out_specs=out_specs,
        out_shape=out_shape,
        scratch_shapes=[pltpu.VMEM((1, LANES), F32)],
        compiler_params=pltpu.CompilerParams(
            dimension_semantics=("parallel", "arbitrary"),
            vmem_limit_bytes=VMEM_LIMIT),
        name="in_proj",
    )(x, ada3, pos3, *consts)


def _attn_kernel(flag_ref, qa_ref, ka_ref, vat_ref, ga_ref, qb_ref, kb_ref, vbt_ref,
                 gb_ref, oa_ref, ob_ref, *, tq, n_q_tiles):
    half = tq // 2
    nt = (((1,), (1,)), ((), ()))
    key = lax.broadcasted_iota(jnp.int32, (half, half), 0)
    qry = lax.broadcasted_iota(jnp.int32, (half, half), 1)
    ones = jnp.ones((BF16_ROWS, n_q_tiles * tq), BF16)
    branches = [(qa_ref, ka_ref, vat_ref, ga_ref, oa_ref, (key // CHUNK) <= (qry // CHUNK)),
                (qb_ref, kb_ref, vbt_ref, gb_ref, ob_ref, key <= qry)]
    chains = [(br, hh) for br in range(2) for hh in range(2)]

    def lanes(hh):
        return slice(hh * HEAD_PAD, (hh + 1) * HEAD_PAD)

    def values(br, hh, lo, hi):
        vt = branches[br][2][0, hh * V_DIM:(hh + 1) * V_DIM, lo:hi]
        return jnp.concatenate([vt, ones[:, :hi - lo]], axis=0)

    def probs(st, m, in_f32):
        if in_f32:
            return jnp.exp2(st).astype(BF16)
        return jnp.exp2((st - m).astype(BF16))

    def full_scores(br, hh, t, j):
        q_ref, k_ref = branches[br][:2]
        st = lax.dot_general(k_ref[0, j * tq:(j + 1) * tq, lanes(hh)],
                             q_ref[0, t * tq:(t + 1) * tq, lanes(hh)], nt,
                             preferred_element_type=F32)
        return st, jnp.max(st, axis=0, keepdims=True)

    def full_pv(br, hh, j, st, m, in_f32=False):
        return jnp.dot(values(br, hh, j * tq, (j + 1) * tq), probs(st, m, in_f32),
                       preferred_element_type=F32)

    def diag_scores(br, hh, t):
        q_ref, k_ref = branches[br][:2]
        mask = branches[br][5]
        base = t * tq
        st_lo = lax.dot_general(k_ref[0, base:base + half, lanes(hh)],
                                q_ref[0, base:base + tq, lanes(hh)], nt,
                                preferred_element_type=F32)
        st_hi = lax.dot_general(k_ref[0, base + half:base + tq, lanes(hh)],
                                q_ref[0, base + half:base + tq, lanes(hh)], nt,
                                preferred_element_type=F32)
        st_lo = jnp.concatenate(
            [jnp.where(mask, st_lo[:, :half], -jnp.inf), st_lo[:, half:]], axis=1)
        st_hi = jnp.where(mask, st_hi, -jnp.inf)
        mx_lo = jnp.max(st_lo, axis=0, keepdims=True)
        mx_hi = jnp.max(st_hi, axis=0, keepdims=True)
        mx = jnp.concatenate(
            [mx_lo[:, :half], jnp.maximum(mx_lo[:, half:], mx_hi)], axis=1)
        return (st_lo, st_hi), mx

    def diag_pv(br, hh, t, st, m, in_f32=False):
        st_lo, st_hi = st
        base = t * tq
        pv = jnp.dot(values(br, hh, base, base + half), probs(st_lo, m, in_f32),
                     preferred_element_type=F32)
        pv_hi = jnp.dot(values(br, hh, base + half, base + tq),
                        probs(st_hi, m[:, half:], in_f32), preferred_element_type=F32)
        return jnp.concatenate([pv[:, :half], pv[:, half:] + pv_hi], axis=1)

    def scores(br, hh, t, j):
        return diag_scores(br, hh, t) if j == t else full_scores(br, hh, t, j)

    def update(br, hh, t, j, st, mx, state):
        m_new = mx if state is None else jnp.maximum(state[0], mx)
        pv = diag_pv(br, hh, t, st, m_new) if j == t else full_pv(br, hh, j, st, m_new)
        if state is None:
            return m_new, pv
        m, acc = state
        return m_new, jnp.exp2(m - m_new) * acc + pv

    def finish(t, accs):
        for br in range(2):
            g_ref, o_ref = branches[br][3:5]
            halves = [accs[(br, hh)] for hh in range(2)]
            halves = [acc[:V_DIM] / acc[V_DIM:V_DIM + 1] for acc in halves]
            o = jnp.concatenate(halves, axis=0).T
            g = g_ref[0, t * tq:(t + 1) * tq, :].astype(F32)
            o_ref[0, t * tq:(t + 1) * tq, :] = (o * (g * jax.nn.sigmoid(g))).astype(BF16)

    def shifted_path():
        zero = jnp.zeros((1, tq), F32)

        def past_scores(br, hh, t):
            q_ref, k_ref = branches[br][:2]
            return lax.dot_general(k_ref[0, :t * tq, lanes(hh)],
                                   q_ref[0, t * tq:(t + 1) * tq, lanes(hh)], nt,
                                   preferred_element_type=F32)

        for t in range(n_q_tiles):
            diag = {c: diag_scores(*c, t)[0] for c in chains}
            past = {c: past_scores(*c, t) for c in chains} if t else None
            accs = {}
            for c in chains:
                acc = diag_pv(*c, t, diag[c], zero, in_f32=True)
                if t:
                    acc = acc + jnp.dot(values(*c, 0, t * tq), probs(past[c], zero, True),
                                        preferred_element_type=F32)
                accs[c] = acc
            finish(t, accs)

    def online_path():
        for t in range(n_q_tiles):
            state = {c: None for c in chains}
            nxt = {c: scores(*c, t, 0) for c in chains}
            for j in range(t + 1):
                cur = nxt
                if j < t:
                    nxt = {c: scores(*c, t, j + 1) for c in chains}
                state = {c: update(*c, t, j, *cur[c], state[c]) for c in chains}
            finish(t, {c: state[c][1] for c in chains})

    pl.when(flag_ref[0] != 0)(shifted_path)
    pl.when(flag_ref[0] == 0)(online_path)


def _attention(flag, qa, ka, vat, ga, qb, kb, vbt, gb, *, tq):
    bsz, seq, _ = qa.shape
    pairs = N_HEADS // 2
    kern = functools.partial(_attn_kernel, tq=tq, n_q_tiles=seq // tq)
    qk_spec = pl.BlockSpec((1, seq, 2 * HEAD_PAD), lambda b, p: (b, 0, p))
    vt_spec = pl.BlockSpec((1, 2 * V_DIM, seq), lambda b, p: (b, p, 0))
    tok_spec = pl.BlockSpec((1, seq, 2 * V_DIM), lambda b, p: (b, 0, p))
    out = jax.ShapeDtypeStruct((bsz, seq, WIDTH), BF16)
    return pl.pallas_call(
        kern,
        grid=(bsz, pairs),
        in_specs=[pl.BlockSpec(memory_space=pltpu.SMEM)]
        + 2 * [qk_spec, qk_spec, vt_spec, tok_spec],
        out_specs=[tok_spec, tok_spec],
        out_shape=[out, out],
        compiler_params=pltpu.CompilerParams(
            dimension_semantics=("parallel", "arbitrary"),
            vmem_limit_bytes=VMEM_LIMIT),
        name="attn",
    )(flag, qa, ka, vat, ga, qb, kb, vbt, gb)


def _out_kernel(x_ref, ada_ref, ya_ref, yb_ref, ma_ref, mb_ref, wa_ref, wb_ref,
                wo_ref, o_ref):
    for r in range(0, x_ref.shape[1], OUT_ROWS):
        rows = slice(r, r + OUT_ROWS)
        pa = jnp.dot(ya_ref[0, rows], wa_ref[...], preferred_element_type=F32)
        pb = jnp.dot(yb_ref[0, rows], wb_ref[...], preferred_element_type=F32)
        merged = (jax.nn.sigmoid(ma_ref[0, rows].astype(F32)) * pa
                  + jax.nn.sigmoid(mb_ref[0, rows].astype(F32)) * pb)
        upd = jnp.dot(merged.astype(BF16), wo_ref[...], preferred_element_type=F32)
        o_ref[0, rows] = x_ref[0, rows] + ada_ref[0, 2:3, :] * upd


def _out_proj(x, ada3, ya, yb, ma, mb, wa, wb, wo, tm):
    bsz, seq, d = x.shape
    tok = lambda width: pl.BlockSpec((1, tm, width), lambda b, i: (b, i, 0))
    return pl.pallas_call(
        _out_kernel,
        grid=(bsz, seq // tm),
        in_specs=[tok(d), pl.BlockSpec((1, 3, d), lambda b, i: (b, 0, 0)),
                  tok(WIDTH), tok(WIDTH), tok(d), tok(d),
                  _const_spec(wa.shape), _const_spec(wb.shape), _const_spec(wo.shape)],
        out_specs=tok(d),
        out_shape=jax.ShapeDtypeStruct((bsz, seq, d), x.dtype),
        compiler_params=pltpu.CompilerParams(
            dimension_semantics=("parallel", "arbitrary"),
            vmem_limit_bytes=VMEM_LIMIT),
        name="out_proj",
    )(x, ada3, ya, yb, ma, mb, wa, wb, wo)


def _tile_heads(vec, pad_to):
    v = jnp.pad(vec.astype(F32), (0, pad_to - vec.shape[0]))
    return jnp.tile(v, N_HEADS).reshape(1, N_HEADS * pad_to)


def _block_avg(size, total):
    gid = jnp.arange(total) // size
    return jnp.where(gid[:, None] == gid[None, :], 1.0 / size, 0.0).astype(BF16)


def _shift_parts(g):
    hi = (-g).astype(BF16).astype(F32)
    mid = (-g - hi).astype(BF16).astype(F32)
    return hi, mid, -g - hi - mid


def kernel(x, c, positions, w_ada, b_ada, norm_w, w_in, b_f, q_lora_norm_w,
           kv_lora_norm_w, w_uq, w_ukv, qn_nope_a, qn_rope_a, kn_nope_a,
           kn_rope_a, qn_b, kn_b, w_branch_a, w_branch_b, w_out):
    bsz, seq, d = x.shape
    depth = w_in.shape[0]
    tm = 512
    tq = 512
    pad = HEAD_PAD - NOPE_DIM - ROPE_DIM

    inv_freq = ROPE_THETA ** (-jnp.arange(0, ROPE_DIM, 2, dtype=F32) / ROPE_DIM)
    invf = inv_freq.reshape(ROPE_DIM // 2, 1)
    pos3 = positions.reshape(bsz, 1, seq)
    g_b = _block_avg(V_DIM, MXU_DIM)
    g_r = _block_avg(ROPE_DIM, MXU_DIM)
    sm_a = LOG2E / math.sqrt(NOPE_DIM + ROPE_DIM)
    sm_b = LOG2E / math.sqrt(V_DIM)

    for l in range(depth):
        ada3 = _ada_proj(c, w_ada[l], b_ada[l]).reshape(bsz, 3, d)

        wi = w_in[l]
        o = [0]
        for s in (Q_LORA, KV_LORA, ROPE_DIM, WIDTH, WIDTH, WIDTH, WIDTH, N_HEADS,
                  WIDTH, D_MODEL, D_MODEL):
            o.append(o[-1] + s)
        (w_cq, w_ckv, w_kr, w_ga, w_qb, w_kb, w_vb, w_f, w_gb, w_ma,
         w_mb) = [wi[:, o[n]:o[n + 1]] for n in range(11)]
        w_krf = jnp.concatenate(
            [w_f, jnp.zeros((d, ROPE_LANE0 - N_HEADS), F32), w_kr,
             jnp.zeros((d, pad), F32)], axis=1)
        win = jnp.concatenate([w_cq, w_ckv, w_krf, w_qb, w_kb, w_vb, w_ga, w_gb,
                               w_ma, w_mb], axis=1).astype(BF16)

        wq = w_uq[l].reshape(Q_LORA, N_HEADS, NOPE_DIM + ROPE_DIM)
        wuqn = wq[:, :, :NOPE_DIM].reshape(Q_LORA, N_HEADS * NOPE_DIM).astype(BF16)
        wuqr = wq[:, :, NOPE_DIM:].reshape(Q_LORA, N_HEADS * ROPE_DIM).astype(BF16)
        wkv = w_ukv[l].reshape(KV_LORA, N_HEADS, NOPE_DIM + V_DIM)
        wuk = wkv[:, :, :NOPE_DIM].reshape(KV_LORA, N_HEADS * NOPE_DIM).astype(BF16)
        wuv = wkv[:, :, NOPE_DIM:].reshape(KV_LORA, WIDTH).astype(BF16)

        qanw = _tile_heads(qn_nope_a[l] * sm_a, NOPE_DIM)
        qarw = _tile_heads(qn_rope_a[l] * sm_a, ROPE_DIM)
        kaw = _tile_heads(kn_nope_a[l], NOPE_DIM)
        krw = jnp.concatenate([jnp.zeros((ROPE_LANE0,), F32), kn_rope_a[l],
                               jnp.zeros((pad,), F32)]).reshape(1, LANES)
        qbw = _tile_heads(qn_b[l] * sm_b, V_DIM)
        kbw = _tile_heads(kn_b[l], V_DIM)
        bfv = jnp.pad(b_f[l].astype(F32), (0, LANES - N_HEADS)).reshape(1, LANES)

        amax = lambda v: jnp.max(jnp.abs(v.astype(F32)))
        bound_a = sm_a * BOUND_SLACK * jnp.sqrt(
            (NOPE_DIM * amax(qn_nope_a[l]) ** 2 + ROPE_DIM * amax(qn_rope_a[l]) ** 2)
            * (NOPE_DIM * amax(kn_nope_a[l]) ** 2 + ROPE_DIM * amax(kn_rope_a[l]) ** 2))
        bound_b = sm_b * BOUND_SLACK * V_DIM * amax(qn_b[l]) * amax(kn_b[l])
        bounded = jnp.maximum(bound_a, bound_b) <= SHIFT_MAX
        shift_a = jnp.where(bounded, bound_a, 0.0)
        shift_b = jnp.where(bounded, bound_b, 0.0)
        ga_shift = jnp.zeros((LANES,), F32).at[SHIFT_LANE0:SHIFT_LANE0 + BIAS_PARTS].set(
            jnp.stack(_shift_parts(shift_a))).reshape(1, LANES)
        gb_shift = jnp.full((1, LANES), shift_b, F32)
        flag = bounded.astype(jnp.int32).reshape(1)

        consts = [norm_w[l].reshape(1, d), win, wuqn, wuqr, wuk, wuv, g_b, g_r,
                  q_lora_norm_w[l].reshape(1, Q_LORA),
                  kv_lora_norm_w[l].reshape(1, KV_LORA), qanw, qarw, kaw, krw, qbw,
                  kbw, invf, bfv, ga_shift, gb_shift]
        (qa, ka, vat, qb, kb, vbt, ga, gb, ma, mb) = _in_proj(x, ada3, pos3, consts, tm)

        ya, yb = _attention(flag, qa, ka, vat, ga, qb, kb, vbt, gb, tq=tq)

        x = _out_proj(x, ada3, ya, yb, ma, mb, w_branch_a[l].astype(BF16),
                      w_branch_b[l].astype(BF16), w_out[l].astype(BF16), 2 * tm)
    return x
```

```python
import functools
import math

import jax
import jax.numpy as jnp
from jax import lax
from jax.experimental import pallas as pl
from jax.experimental.pallas import tpu as pltpu

D_MODEL = 1024
CHUNK = 64
N_HEADS = 8
NOPE_DIM = 64
ROPE_DIM = 32
V_DIM = 64
Q_LORA = 384
KV_LORA = 256
WIDTH = N_HEADS * V_DIM
ROPE_THETA = 10000.0
EPS = 1e-6
LOG2E = math.log2(math.e)

LANES = 128
HEAD_PAD = 128
QK_WIDTH = N_HEADS * HEAD_PAD
MXU_DIM = 256
BF16_ROWS = 16
VMEM_LIMIT = 48 * 1024 * 1024
OUT_ROWS = 512

C_CQ = 0
C_CKV = C_CQ + Q_LORA
C_KR = C_CKV + KV_LORA
C_QB = C_KR + LANES
C_KB = C_QB + WIDTH
C_VB = C_KB + WIDTH
C_GA = C_VB + WIDTH
C_GB = C_GA + WIDTH
C_MA = C_GB + WIDTH
C_MB = C_MA + D_MODEL
C_END = C_MB + D_MODEL
ROPE_LANE0 = NOPE_DIM
BIAS_LANE0 = V_DIM
BIAS_PARTS = 3
SHIFT_LANE0 = NOPE_DIM + ROPE_DIM
SHIFT_MAX = 48.0
BOUND_SLACK = 1.02

BF16 = jnp.bfloat16
F32 = jnp.float32


def _const_spec(shape):
    nd = len(shape)
    return pl.BlockSpec(shape, lambda *_: (0,) * nd, pipeline_mode=pl.Buffered(1))


def _ada_kernel(c_ref, w_ref, b_ref, o_ref):
    o_ref[...] = jnp.dot(c_ref[...].astype(BF16), w_ref[...].astype(BF16),
                         preferred_element_type=F32) + b_ref[...]


def _ada_proj(c, w_ada, b_ada):
    bsz, d = c.shape
    n = w_ada.shape[1]
    tn = 512
    return pl.pallas_call(
        _ada_kernel,
        grid=(n // tn,),
        in_specs=[pl.BlockSpec((bsz, d), lambda j: (0, 0)),
                  pl.BlockSpec((d, tn), lambda j: (0, j)),
                  pl.BlockSpec((1, tn), lambda j: (0, j))],
        out_specs=pl.BlockSpec((bsz, tn), lambda j: (0, j)),
        out_shape=jax.ShapeDtypeStruct((bsz, n), F32),
        name="ada_proj",
    )(c, w_ada, b_ada.reshape(1, n))


def _rms(x, n):
    return lax.rsqrt(jnp.sum(x * x, axis=-1, keepdims=True) * (1.0 / n) + EPS)


def _group_mean_sq(x, g_ref):
    g = g_ref[...]
    sq = (x * x).astype(BF16)
    parts = [jnp.dot(sq[:, c:c + MXU_DIM], g, preferred_element_type=F32)
             for c in range(0, x.shape[1], MXU_DIM)]
    return jnp.concatenate(parts, axis=1)


def _rot_half(x):
    lane = lax.broadcasted_iota(jnp.int32, x.shape, 1)
    half = ROPE_DIM // 2
    return jnp.where((lane & half) == 0,
                     pltpu.roll(x, LANES - half, axis=1),
                     pltpu.roll(x, half, axis=1))


def _head_tiles(x):
    tiles = []
    for hd in range(N_HEADS):
        blk = x[:, (hd // 2) * LANES:(hd // 2 + 1) * LANES]
        tiles.append(pltpu.roll(blk, V_DIM, axis=1) if hd % 2 else blk)
    return tiles


def _in_proj_kernel(x_ref, ada_ref, pos_ref, nw_ref, win_ref, wuqn_ref, wuqr_ref,
                    wuk_ref, wuv_ref, gb_mat_ref, gr_mat_ref, qlw_ref, kvlw_ref,
                    qanw_ref, qarw_ref, kaw_ref, krw_ref, qbw_ref, kbw_ref, invf_ref,
                    bf_ref, ga_shift_ref, gb_shift_ref,
                    qa_ref, ka_ref, vat_ref, qb_ref, kb_ref, vbt_ref,
                    ga_ref, gb_ref, ma_ref, mb_ref, carry_ref):
    tm = x_ref.shape[1]
    x = x_ref[0]
    shift = ada_ref[0, 0:1, :]
    scale = ada_ref[0, 1:2, :]
    h = (x * _rms(x, D_MODEL)) * (nw_ref[...] * (1.0 + scale)) + shift
    hb = h.astype(BF16)

    def proj(lo, hi):
        return jnp.dot(hb, win_ref[:, lo:hi], preferred_element_type=F32)

    pieces = [(ga_ref, C_GA, C_GB), (gb_ref, C_GB, C_MA), (ma_ref, C_MA, C_MB),
              (mb_ref, C_MB, C_END)]

    def fill(n=1):
        for _ in range(min(n, len(pieces))):
            ref, lo, hi = pieces.pop(0)
            ref[0] = proj(lo, hi).astype(BF16)

    ang = invf_ref[...] * pos_ref[0].astype(F32)
    cos_h = jnp.cos(ang)
    sin_h = jnp.sin(ang)
    groups = LANES // ROPE_DIM
    cos_t = jnp.concatenate(groups * [cos_h, cos_h], axis=0).T
    sin_t = jnp.concatenate(groups * [-sin_h, sin_h], axis=0).T
    lane = lax.broadcasted_iota(jnp.int32, (tm, LANES), 1)

    def rope(t):
        return t * cos_t + _rot_half(t) * sin_t

    lat = proj(C_CQ, C_QB)
    cq = lat[:, C_CQ:C_CKV]
    ckv = lat[:, C_CKV:C_KR]
    kr = lat[:, C_KR:C_QB]
    fill(2)

    cqn = (cq * _rms(cq, Q_LORA) * qlw_ref[...]).astype(BF16)
    qnp = jnp.dot(cqn, wuqn_ref[...], preferred_element_type=F32)
    qrp = jnp.dot(cqn, wuqr_ref[...], preferred_element_type=F32)
    qnp = qnp * lax.rsqrt(_group_mean_sq(qnp, gb_mat_ref) + EPS) * qanw_ref[...]
    qrp = qrp * lax.rsqrt(_group_mean_sq(qrp, gr_mat_ref) + EPS) * qarw_ref[...]
    qrp = [rope(qrp[:, c:c + LANES]) for c in range(0, qrp.shape[1], LANES)]
    head_lane = lane < V_DIM
    rope_lane = (lane >= ROPE_LANE0) & (lane < ROPE_LANE0 + ROPE_DIM)
    shift_lane = (lane >= SHIFT_LANE0) & (lane < SHIFT_LANE0 + BIAS_PARTS)
    ones_a = jnp.where(shift_lane, 1.0, 0.0)
    for hd, t in enumerate(_head_tiles(qnp)):
        grp = hd % groups
        rp = qrp[hd // groups]
        move = (ROPE_LANE0 - grp * ROPE_DIM) % LANES
        rp = pltpu.roll(rp, move, axis=1) if move else rp
        qa_ref[0, :, hd * HEAD_PAD:(hd + 1) * HEAD_PAD] = jnp.where(
            head_lane, t, jnp.where(rope_lane, rp, ones_a)).astype(BF16)
    fill()

    ckvn = (ckv * _rms(ckv, KV_LORA) * kvlw_ref[...]).astype(BF16)
    knp = jnp.dot(ckvn, wuk_ref[...], preferred_element_type=F32)
    knp = knp * lax.rsqrt(_group_mean_sq(knp, gb_mat_ref) + EPS) * kaw_ref[...]
    kr_ms = jnp.sum(jnp.where(rope_lane, kr * kr, 0.0), axis=-1, keepdims=True)
    krn = kr * lax.rsqrt(kr_ms * (1.0 / ROPE_DIM) + EPS) * krw_ref[...]
    krr = rope(krn) + ga_shift_ref[...]
    for hd, t in enumerate(_head_tiles(knp)):
        ka_ref[0, :, hd * HEAD_PAD:(hd + 1) * HEAD_PAD] = jnp.where(
            head_lane, t, krr).astype(BF16)
    va = jnp.dot(ckvn, wuv_ref[...], preferred_element_type=F32)
    vat_ref[0] = va.T.astype(BF16)
    fill()

    z = kr.T[:N_HEADS] + bf_ref[...]
    c = jnp.minimum(z, 0.0) - jnp.log1p(jnp.exp(-jnp.abs(z)))
    tok = lax.broadcasted_iota(jnp.int32, c.shape, 1)
    step = 1
    while step < tm:
        c = c + jnp.where(tok >= step, pltpu.roll(c, step, axis=1), 0.0)
        step *= 2

    @pl.when(pl.program_id(1) == 0)
    def _():
        carry_ref[...] = jnp.zeros_like(carry_ref)

    c = c + carry_ref[:, :1]
    carry_ref[...] = jnp.broadcast_to(c[:, tm - 1:tm], carry_ref.shape)
    c2 = c * LOG2E

    def parts(v):
        hi = v.astype(BF16).astype(F32)
        mid = (v - hi).astype(BF16).astype(F32)
        return [hi, mid, v - hi - mid]

    q_part_lane0 = BIAS_LANE0 + BIAS_PARTS * N_HEADS
    used = q_part_lane0 + BIAS_PARTS * N_HEADS
    shared = jnp.concatenate(
        [jnp.zeros((BIAS_LANE0, tm), F32)] + parts(-c2 - gb_shift_ref[:, :1]) + parts(c2)
        + [jnp.zeros((LANES - used, tm), F32)], axis=0).T
    in_k_parts = (lane >= BIAS_LANE0) & (lane < q_part_lane0)
    in_q_parts = (lane >= q_part_lane0) & (lane < q_part_lane0 + BIAS_PARTS * N_HEADS)
    k_tail = jnp.where(in_k_parts, shared, 0.0)
    q_tail = jnp.where(in_q_parts, shared, 0.0)
    lane_row = lane[:1]
    own = [(lane_row & (N_HEADS - 1)) == hd for hd in range(N_HEADS)]
    ones_q = [jnp.where(in_k_parts[:1] & o, 1.0, 0.0) for o in own]
    ones_k = [jnp.where(in_q_parts[:1] & o, 1.0, 0.0) for o in own]

    qb = proj(C_QB, C_KB)
    qbn = qb * lax.rsqrt(_group_mean_sq(qb, gb_mat_ref) + EPS) * qbw_ref[...]
    for hd, t in enumerate(_head_tiles(qbn)):
        qb_ref[0, :, hd * HEAD_PAD:(hd + 1) * HEAD_PAD] = jnp.where(
            head_lane, t, q_tail + ones_q[hd]).astype(BF16)
    kb = proj(C_KB, C_VB)
    kbn = kb * lax.rsqrt(_group_mean_sq(kb, gb_mat_ref) + EPS) * kbw_ref[...]
    for hd, t in enumerate(_head_tiles(kbn)):
        kb_ref[0, :, hd * HEAD_PAD:(hd + 1) * HEAD_PAD] = jnp.where(
            head_lane, t, k_tail + ones_k[hd]).astype(BF16)
    vbt_ref[0] = proj(C_VB, C_GA).T.astype(BF16)
    assert not pieces


def _in_proj(x, ada3, pos3, consts, tm):
    bsz, seq, d = x.shape
    grid = (bsz, seq // tm)
    tok = lambda width: pl.BlockSpec((1, tm, width), lambda b, i: (b, i, 0))
    tok_t = pl.BlockSpec((1, WIDTH, tm), lambda b, i: (b, 0, i))
    in_specs = [tok(d),
                pl.BlockSpec((1, 3, d), lambda b, i: (b, 0, 0)),
                pl.BlockSpec((1, 1, tm), lambda b, i: (b, 0, i))]
    in_specs += [_const_spec(a.shape) for a in consts]
    row_major = lambda w: jax.ShapeDtypeStruct((bsz, seq, w), BF16)
    transposed = jax.ShapeDtypeStruct((bsz, WIDTH, seq), BF16)
    out_shape = [row_major(QK_WIDTH), row_major(QK_WIDTH), transposed,
                 row_major(QK_WIDTH), row_major(QK_WIDTH), transposed,
                 row_major(WIDTH), row_major(WIDTH), row_major(d), row_major(d)]
    out_specs = [tok(QK_WIDTH), tok(QK_WIDTH), tok_t,
                 tok(QK_WIDTH), tok(QK_WIDTH), tok_t,
                 tok(WIDTH), tok(WIDTH), tok(d), tok(d)]
    return pl.pallas_call(
        _in_proj_kernel,
        grid=grid,
        in_specs=in_specs,
        out_specs=out_specs,
        out_shape=out_shape,
        scratch_shapes=[pltpu.VMEM((N_HEADS, LANES), F32)],
        compiler_params=pltpu.CompilerParams(
            dimension_semantics=("parallel", "arbitrary"),
            vmem_limit_bytes=VMEM_LIMIT),
        name="in_proj",
    )(x, ada3, pos3, *consts)


def _attn_kernel(flag_ref, qa_ref, ka_ref, vat_ref, ga_ref, qb_ref, kb_ref, vbt_ref,
                 gb_ref, oa_ref, ob_ref, *, tq, n_q_tiles):
    half = tq // 2
    nt = (((1,), (1,)), ((), ()))
    key = lax.broadcasted_iota(jnp.int32, (half, half), 0)
    qry = lax.broadcasted_iota(jnp.int32, (half, half), 1)
    ones = jnp.ones((BF16_ROWS, n_q_tiles * tq), BF16)
    branches = [(qa_ref, ka_ref, vat_ref, ga_ref, oa_ref, (key // CHUNK) <= (qry // CHUNK)),
                (qb_ref, kb_ref, vbt_ref, gb_ref, ob_ref, key <= qry)]
    chains = [(br, hh) for br in range(2) for hh in range(2)]

    def lanes(hh):
        return slice(hh * HEAD_PAD, (hh + 1) * HEAD_PAD)

    def values(br, hh, lo, hi):
        vt = branches[br][2][0, hh * V_DIM:(hh + 1) * V_DIM, lo:hi]
        return jnp.concatenate([vt, ones[:, :hi - lo]], axis=0)

    def probs(st, m, in_f32):
        if in_f32:
            return jnp.exp2(st).astype(BF16)
        return jnp.exp2((st - m).astype(BF16))

    def full_scores(br, hh, t, j):
        q_ref, k_ref = branches[br][:2]
        st = lax.dot_general(k_ref[0, j * tq:(j + 1) * tq, lanes(hh)],
                             q_ref[0, t * tq:(t + 1) * tq, lanes(hh)], nt,
                             preferred_element_type=F32)
        return st, jnp.max(st, axis=0, keepdims=True)

    def full_pv(br, hh, j, st, m, in_f32=False):
        return jnp.dot(values(br, hh, j * tq, (j + 1) * tq), probs(st, m, in_f32),
                       preferred_element_type=F32)

    def diag_scores(br, hh, t):
        q_ref, k_ref = branches[br][:2]
        mask = branches[br][5]
        base = t * tq
        st_lo = lax.dot_general(k_ref[0, base:base + half, lanes(hh)],
                                q_ref[0, base:base + tq, lanes(hh)], nt,
                                preferred_element_type=F32)
        st_hi = lax.dot_general(k_ref[0, base + half:base + tq, lanes(hh)],
                                q_ref[0, base + half:base + tq, lanes(hh)], nt,
                                preferred_element_type=F32)
        st_lo = jnp.concatenate(
            [jnp.where(mask, st_lo[:, :half], -jnp.inf), st_lo[:, half:]], axis=1)
        st_hi = jnp.where(mask, st_hi, -jnp.inf)
        mx_lo = jnp.max(st_lo, axis=0, keepdims=True)
        mx_hi = jnp.max(st_hi, axis=0, keepdims=True)
        mx = jnp.concatenate(
            [mx_lo[:, :half], jnp.maximum(mx_lo[:, half:], mx_hi)], axis=1)
        return (st_lo, st_hi), mx

    def diag_pv(br, hh, t, st, m, in_f32=False):
        st_lo, st_hi = st
        base = t * tq
        pv = jnp.dot(values(br, hh, base, base + half), probs(st_lo, m, in_f32),
                     preferred_element_type=F32)
        pv_hi = jnp.dot(values(br, hh, base + half, base + tq),
                        probs(st_hi, m[:, half:], in_f32), preferred_element_type=F32)
        return jnp.concatenate([pv[:, :half], pv[:, half:] + pv_hi], axis=1)

    def scores(br, hh, t, j):
        return diag_scores(br, hh, t) if j == t else full_scores(br, hh, t, j)

    def update(br, hh, t, j, st, mx, state):
        m_new = mx if state is None else jnp.maximum(state[0], mx)
        pv = diag_pv(br, hh, t, st, m_new) if j == t else full_pv(br, hh, j, st, m_new)
        if state is None:
            return m_new, pv
        m, acc = state
        return m_new, jnp.exp2(m - m_new) * acc + pv

    def finish(t, accs):
        for br in range(2):
            g_ref, o_ref = branches[br][3:5]
            halves = [accs[(br, hh)] for hh in range(2)]
            halves = [acc[:V_DIM] / acc[V_DIM:V_DIM + 1] for acc in halves]
            o = jnp.concatenate(halves, axis=0).T
            g = g_ref[0, t * tq:(t + 1) * tq, :].astype(F32)
            o_ref[0, t * tq:(t + 1) * tq, :] = (o * (g * jax.nn.sigmoid(g))).astype(BF16)

    def shifted_path():
        zero = jnp.zeros((1, tq), F32)

        def past_scores(br, hh, t):
            q_ref, k_ref = branches[br][:2]
            return lax.dot_general(k_ref[0, :t * tq, lanes(hh)],
                                   q_ref[0, t * tq:(t + 1) * tq, lanes(hh)], nt,
                                   preferred_element_type=F32)

        for t in range(n_q_tiles):
            diag = {c: diag_scores(*c, t)[0] for c in chains}
            past = {c: past_scores(*c, t) for c in chains} if t else None
            accs = {}
            for c in chains:
                acc = diag_pv(*c, t, diag[c], zero, in_f32=True)
                if t:
                    acc = acc + jnp.dot(values(*c, 0, t * tq), probs(past[c], zero, True),
                                        preferred_element_type=F32)
                accs[c] = acc
            finish(t, accs)

    def online_path():
        for t in range(n_q_tiles):
            state = {c: None for c in chains}
            nxt = {c: scores(*c, t, 0) for c in chains}
            for j in range(t + 1):
                cur = nxt
                if j < t:
                    nxt = {c: scores(*c, t, j + 1) for c in chains}
                state = {c: update(*c, t, j, *cur[c], state[c]) for c in chains}
            finish(t, {c: state[c][1] for c in chains})

    pl.when(flag_ref[0] != 0)(shifted_path)
    pl.when(flag_ref[0] == 0)(online_path)


def _attention(flag, qa, ka, vat, ga, qb, kb, vbt, gb, *, tq):
    bsz, seq, _ = qa.shape
    pairs = N_HEADS // 2
    kern = functools.partial(_attn_kernel, tq=tq, n_q_tiles=seq // tq)
    qk_spec = pl.BlockSpec((1, seq, 2 * HEAD_PAD), lambda b, p: (b, 0, p))
    vt_spec = pl.BlockSpec((1, 2 * V_DIM, seq), lambda b, p: (b, p, 0))
    tok_spec = pl.BlockSpec((1, seq, 2 * V_DIM), lambda b, p: (b, 0, p))
    out = jax.ShapeDtypeStruct((bsz, seq, WIDTH), BF16)
    return pl.pallas_call(
        kern,
        grid=(bsz, pairs),
        in_specs=[pl.BlockSpec(memory_space=pltpu.SMEM)]
        + 2 * [qk_spec, qk_spec, vt_spec, tok_spec],
        out_specs=[tok_spec, tok_spec],
        out_shape=[out, out],
        compiler_params=pltpu.CompilerParams(
            dimension_semantics=("parallel", "arbitrary"),
            vmem_limit_bytes=VMEM_LIMIT),
        name="attn",
    )(flag, qa, ka, vat, ga, qb, kb, vbt, gb)


def _out_kernel(x_ref, ada_ref, ya_ref, yb_ref, ma_ref, mb_ref, wa_ref, wb_ref,
                wo_ref, o_ref):
    for r in range(0, x_ref.shape[1], OUT_ROWS):
        rows = slice(r, r + OUT_ROWS)
        pa = jnp.dot(ya_ref[0, rows], wa_ref[...], preferred_element_type=F32)
        pb = jnp.dot(yb_ref[0, rows], wb_ref[...], preferred_element_type=F32)
        merged = (jax.nn.sigmoid(ma_ref[0, rows].astype(F32)) * pa
                  + jax.nn.sigmoid(mb_ref[0, rows].astype(F32)) * pb)
        upd = jnp.dot(merged.astype(BF16), wo_ref[...], preferred_element_type=F32)
        o_ref[0, rows] = x_ref[0, rows] + ada_ref[0, 2:3, :] * upd


def _out_proj(x, ada3, ya, yb, ma, mb, wa, wb, wo, tm):
    bsz, seq, d = x.shape
    tok = lambda width: pl.BlockSpec((1, tm, width), lambda b, i: (b, i, 0))
    return pl.pallas_call(
        _out_kernel,
        grid=(bsz, seq // tm),
        in_specs=[tok(d), pl.BlockSpec((1, 3, d), lambda b, i: (b, 0, 0)),
                  tok(WIDTH), tok(WIDTH), tok(d), tok(d),
                  _const_spec(wa.shape), _const_spec(wb.shape), _const_spec(wo.shape)],
        out_specs=tok(d),
        out_shape=jax.ShapeDtypeStruct((bsz, seq, d), x.dtype),
        compiler_params=pltpu.CompilerParams(
            dimension_semantics=("parallel", "arbitrary"),
            vmem_limit_bytes=VMEM_LIMIT),
        name="out_proj",
    )(x, ada3, ya, yb, ma, mb, wa, wb, wo)


def _tile_heads(vec, pad_to):
    v = jnp.pad(vec.astype(F32), (0, pad_to - vec.shape[0]))
    return jnp.tile(v, N_HEADS).reshape(1, N_HEADS * pad_to)


def _block_avg(size, total):
    gid = jnp.arange(total) // size
    return jnp.where(gid[:, None] == gid[None, :], 1.0 / size, 0.0).astype(BF16)


def _shift_parts(g):
    hi = (-g).astype(BF16).astype(F32)
    mid = (-g - hi).astype(BF16).astype(F32)
    return hi, mid, -g - hi - mid


def kernel(x, c, positions, w_ada, b_ada, norm_w, w_in, b_f, q_lora_norm_w,
           kv_lora_norm_w, w_uq, w_ukv, qn_nope_a, qn_rope_a, kn_nope_a,
           kn_rope_a, qn_b, kn_b, w_branch_a, w_branch_b, w_out):
    bsz, seq, d = x.shape
    depth = w_in.shape[0]
    tm = 512
    tq = 512
    pad = HEAD_PAD - NOPE_DIM - ROPE_DIM

    inv_freq = ROPE_THETA ** (-jnp.arange(0, ROPE_DIM, 2, dtype=F32) / ROPE_DIM)
    invf = inv_freq.reshape(ROPE_DIM // 2, 1)
    pos3 = positions.reshape(bsz, 1, seq)
    g_b = _block_avg(V_DIM, MXU_DIM)
    g_r = _block_avg(ROPE_DIM, MXU_DIM)
    sm_a = LOG2E / math.sqrt(NOPE_DIM + ROPE_DIM)
    sm_b = LOG2E / math.sqrt(V_DIM)

    for l in range(depth):
        ada3 = _ada_proj(c, w_ada[l], b_ada[l]).reshape(bsz, 3, d)

        wi = w_in[l]
        o = [0]
        for s in (Q_LORA, KV_LORA, ROPE_DIM, WIDTH, WIDTH, WIDTH, WIDTH, N_HEADS,
                  WIDTH, D_MODEL, D_MODEL):
            o.append(o[-1] + s)
        (w_cq, w_ckv, w_kr, w_ga, w_qb, w_kb, w_vb, w_f, w_gb, w_ma,
         w_mb) = [wi[:, o[n]:o[n + 1]] for n in range(11)]
        w_krf = jnp.concatenate(
            [w_f, jnp.zeros((d, ROPE_LANE0 - N_HEADS), F32), w_kr,
             jnp.zeros((d, pad), F32)], axis=1)
        win = jnp.concatenate([w_cq, w_ckv, w_krf, w_qb, w_kb, w_vb, w_ga, w_gb,
                               w_ma, w_mb], axis=1).astype(BF16)

        wq = w_uq[l].reshape(Q_LORA, N_HEADS, NOPE_DIM + ROPE_DIM)
        wuqn = wq[:, :, :NOPE_DIM].reshape(Q_LORA, N_HEADS * NOPE_DIM).astype(BF16)
        wuqr = wq[:, :, NOPE_DIM:].reshape(Q_LORA, N_HEADS * ROPE_DIM).astype(BF16)
        wkv = w_ukv[l].reshape(KV_LORA, N_HEADS, NOPE_DIM + V_DIM)
        wuk = wkv[:, :, :NOPE_DIM].reshape(KV_LORA, N_HEADS * NOPE_DIM).astype(BF16)
        wuv = wkv[:, :, NOPE_DIM:].reshape(KV_LORA, WIDTH).astype(BF16)

        qanw = _tile_heads(qn_nope_a[l] * sm_a, NOPE_DIM)
        qarw = _tile_heads(qn_rope_a[l] * sm_a, ROPE_DIM)
        kaw = _tile_heads(kn_nope_a[l], NOPE_DIM)
        krw = jnp.concatenate([jnp.zeros((ROPE_LANE0,), F32), kn_rope_a[l],
                               jnp.zeros((pad,), F32)]).reshape(1, LANES)
        qbw = _tile_heads(qn_b[l] * sm_b, V_DIM)
        kbw = _tile_heads(kn_b[l], V_DIM)
        bfv = b_f[l].astype(F32).reshape(N_HEADS, 1)

        amax = lambda v: jnp.max(jnp.abs(v.astype(F32)))
        bound_a = sm_a * BOUND_SLACK * jnp.sqrt(
            (NOPE_DIM * amax(qn_nope_a[l]) ** 2 + ROPE_DIM * amax(qn_rope_a[l]) ** 2)
            * (NOPE_DIM * amax(kn_nope_a[l]) ** 2 + ROPE_DIM * amax(kn_rope_a[l]) ** 2))
        bound_b = sm_b * BOUND_SLACK * V_DIM * amax(qn_b[l]) * amax(kn_b[l])
        bounded = jnp.maximum(bound_a, bound_b) <= SHIFT_MAX
        shift_a = jnp.where(bounded, bound_a, 0.0)
        shift_b = jnp.where(bounded, bound_b, 0.0)
        ga_shift = jnp.zeros((LANES,), F32).at[SHIFT_LANE0:SHIFT_LANE0 + BIAS_PARTS].set(
            jnp.stack(_shift_parts(shift_a))).reshape(1, LANES)
        gb_shift = jnp.full((1, LANES), shift_b, F32)
        flag = bounded.astype(jnp.int32).reshape(1)

        consts = [norm_w[l].reshape(1, d), win, wuqn, wuqr, wuk, wuv, g_b, g_r,
                  q_lora_norm_w[l].reshape(1, Q_LORA),
                  kv_lora_norm_w[l].reshape(1, KV_LORA), qanw, qarw, kaw, krw, qbw,
                  kbw, invf, bfv, ga_shift, gb_shift]
        (qa, ka, vat, qb, kb, vbt, ga, gb, ma, mb) = _in_proj(x, ada3, pos3, consts, tm)

        ya, yb = _attention(flag, qa, ka, vat, ga, qb, kb, vbt, gb, tq=tq)

        x = _out_proj(x, ada3, ya, yb, ma, mb, w_branch_a[l].astype(BF16),
                      w_branch_b[l].astype(BF16), w_out[l].astype(BF16), 2 * tm)
    return x
```

```python
import functools
import math

import jax
import jax.numpy as jnp
from jax import lax
from jax.experimental import pallas as pl
from jax.experimental.pallas import tpu as pltpu

D_MODEL = 1024
CHUNK = 64
N_HEADS = 8
NOPE_DIM = 64
ROPE_DIM = 32
V_DIM = 64
Q_LORA = 384
KV_LORA = 256
WIDTH = N_HEADS * V_DIM
ROPE_THETA = 10000.0
EPS = 1e-6
LOG2E = math.log2(math.e)

LANES = 128
HEAD_PAD = 128
QK_WIDTH = N_HEADS * HEAD_PAD
MXU_DIM = 256
BF16_ROWS = 16
VMEM_LIMIT = 48 * 1024 * 1024
OUT_ROWS = 512

C_CQ = 0
C_CKV = C_CQ + Q_LORA
C_KR = C_CKV + KV_LORA
C_QB = C_KR + LANES
C_KB = C_QB + WIDTH
C_VB = C_KB + WIDTH
C_GA = C_VB + WIDTH
C_GB = C_GA + WIDTH
C_MA = C_GB + WIDTH
C_MB = C_MA + D_MODEL
C_END = C_MB + D_MODEL
ROPE_LANE0 = NOPE_DIM
BIAS_LANE0 = V_DIM
BIAS_PARTS = 3
SHIFT_LANE0 = NOPE_DIM + ROPE_DIM
SHIFT_MAX = 48.0
BOUND_SLACK = 1.02

BF16 = jnp.bfloat16
F32 = jnp.float32


def _const_spec(shape):
    nd = len(shape)
    return pl.BlockSpec(shape, lambda *_: (0,) * nd, pipeline_mode=pl.Buffered(1))


def _ada_kernel(c_ref, w_ref, b_ref, o_ref):
    o_ref[...] = jnp.dot(c_ref[...].astype(BF16), w_ref[...].astype(BF16),
                         preferred_element_type=F32) + b_ref[...]


def _ada_proj(c, w_ada, b_ada):
    bsz, d = c.shape
    n = w_ada.shape[1]
    tn = 512
    return pl.pallas_call(
        _ada_kernel,
        grid=(n // tn,),
        in_specs=[pl.BlockSpec((bsz, d), lambda j: (0, 0)),
                  pl.BlockSpec((d, tn), lambda j: (0, j)),
                  pl.BlockSpec((1, tn), lambda j: (0, j))],
        out_specs=pl.BlockSpec((bsz, tn), lambda j: (0, j)),
        out_shape=jax.ShapeDtypeStruct((bsz, n), F32),
        name="ada_proj",
    )(c, w_ada, b_ada.reshape(1, n))


def _rms(x, n):
    return lax.rsqrt(jnp.sum(x * x, axis=-1, keepdims=True) * (1.0 / n) + EPS)


def _group_mean_sq(x, g_ref):
    g = g_ref[...]
    sq = (x * x).astype(BF16)
    parts = [jnp.dot(sq[:, c:c + MXU_DIM], g, preferred_element_type=F32)
             for c in range(0, x.shape[1], MXU_DIM)]
    return jnp.concatenate(parts, axis=1)


def _rot_half(x):
    lane = lax.broadcasted_iota(jnp.int32, x.shape, 1)
    half = ROPE_DIM // 2
    return jnp.where((lane & half) == 0,
                     pltpu.roll(x, LANES - half, axis=1),
                     pltpu.roll(x, half, axis=1))


def _head_tiles(x):
    tiles = []
    for hd in range(N_HEADS):
        blk = x[:, (hd // 2) * LANES:(hd // 2 + 1) * LANES]
        tiles.append(pltpu.roll(blk, V_DIM, axis=1) if hd % 2 else blk)
    return tiles


def _in_proj_kernel(x_ref, ada_ref, pos_ref, nw_ref, wlat_ref, wqb_ref, wkb_ref, wvb_ref,
                    wga_ref, wgb_ref, wma_ref, wmb_ref, wuqn_ref, wuqr_ref,
                    wuk_ref, wuv_ref, gb_mat_ref, gr_mat_ref, qlw_ref, kvlw_ref,
                    qanw_ref, qarw_ref, kaw_ref, krw_ref, qbw_ref, kbw_ref, invf_ref,
                    bf_ref, ga_shift_ref, gb_shift_ref,
                    qa_ref, ka_ref, vat_ref, qb_ref, kb_ref, vbt_ref,
                    ga_ref, gb_ref, ma_ref, mb_ref, carry_ref):
    tm = x_ref.shape[1]
    x = x_ref[0]
    shift = ada_ref[0, 0:1, :]
    scale = ada_ref[0, 1:2, :]
    h = (x * _rms(x, D_MODEL)) * (nw_ref[...] * (1.0 + scale)) + shift
    hb = h.astype(BF16)

    groups_w = {C_CQ: wlat_ref, C_QB: wqb_ref, C_KB: wkb_ref, C_VB: wvb_ref,
                C_GA: wga_ref, C_GB: wgb_ref, C_MA: wma_ref, C_MB: wmb_ref}

    def proj(lo, hi):
        w_ref = groups_w[lo]
        assert w_ref.shape[1] == hi - lo
        return jnp.dot(hb, w_ref[...], preferred_element_type=F32)

    pieces = [(ga_ref, C_GA, C_GB), (gb_ref, C_GB, C_MA), (ma_ref, C_MA, C_MB),
              (mb_ref, C_MB, C_END)]

    def fill(n=1):
        for _ in range(min(n, len(pieces))):
            ref, lo, hi = pieces.pop(0)
            ref[0] = proj(lo, hi).astype(BF16)

    ang = invf_ref[...] * pos_ref[0].astype(F32)
    cos_h = jnp.cos(ang)
    sin_h = jnp.sin(ang)
    groups = LANES // ROPE_DIM
    cos_t = jnp.concatenate(groups * [cos_h, cos_h], axis=0).T
    sin_t = jnp.concatenate(groups * [-sin_h, sin_h], axis=0).T
    lane = lax.broadcasted_iota(jnp.int32, (tm, LANES), 1)

    def rope(t):
        return t * cos_t + _rot_half(t) * sin_t

    lat = proj(C_CQ, C_QB)
    cq = lat[:, C_CQ:C_CKV]
    ckv = lat[:, C_CKV:C_KR]
    kr = lat[:, C_KR:C_QB]
    fill(2)

    cqn = (cq * _rms(cq, Q_LORA) * qlw_ref[...]).astype(BF16)
    qnp = jnp.dot(cqn, wuqn_ref[...], preferred_element_type=F32)
    qrp = jnp.dot(cqn, wuqr_ref[...], preferred_element_type=F32)
    qnp = qnp * lax.rsqrt(_group_mean_sq(qnp, gb_mat_ref) + EPS) * qanw_ref[...]
    qrp = qrp * lax.rsqrt(_group_mean_sq(qrp, gr_mat_ref) + EPS) * qarw_ref[...]
    qrp = [rope(qrp[:, c:c + LANES]) for c in range(0, qrp.shape[1], LANES)]
    head_lane = lane < V_DIM
    rope_lane = (lane >= ROPE_LANE0) & (lane < ROPE_LANE0 + ROPE_DIM)
    shift_lane = (lane >= SHIFT_LANE0) & (lane < SHIFT_LANE0 + BIAS_PARTS)
    ones_a = jnp.where(shift_lane, 1.0, 0.0)
    for hd, t in enumerate(_head_tiles(qnp)):
        grp = hd % groups
        rp = qrp[hd // groups]
        move = (ROPE_LANE0 - grp * ROPE_DIM) % LANES
        rp = pltpu.roll(rp, move, axis=1) if move else rp
        qa_ref[0, :, hd * HEAD_PAD:(hd + 1) * HEAD_PAD] = jnp.where(
            head_lane, t, jnp.where(rope_lane, rp, ones_a)).astype(BF16)
    fill()

    ckvn = (ckv * _rms(ckv, KV_LORA) * kvlw_ref[...]).astype(BF16)
    knp = jnp.dot(ckvn, wuk_ref[...], preferred_element_type=F32)
    knp = knp * lax.rsqrt(_group_mean_sq(knp, gb_mat_ref) + EPS) * kaw_ref[...]
    kr_ms = jnp.sum(jnp.where(rope_lane, kr * kr, 0.0), axis=-1, keepdims=True)
    krn = kr * lax.rsqrt(kr_ms * (1.0 / ROPE_DIM) + EPS) * krw_ref[...]
    krr = rope(krn) + ga_shift_ref[...]
    for hd, t in enumerate(_head_tiles(knp)):
        ka_ref[0, :, hd * HEAD_PAD:(hd + 1) * HEAD_PAD] = jnp.where(
            head_lane, t, krr).astype(BF16)
    va = jnp.dot(ckvn, wuv_ref[...], preferred_element_type=F32)
    vat_ref[0] = va.T.astype(BF16)
    fill()

    z = kr.T[:N_HEADS] + bf_ref[...]
    c = jnp.minimum(z, 0.0) - jnp.log1p(jnp.exp(-jnp.abs(z)))
    tok = lax.broadcasted_iota(jnp.int32, c.shape, 1)
    step = 1
    while step < tm:
        c = c + jnp.where(tok >= step, pltpu.roll(c, step, axis=1), 0.0)
        step *= 2

    @pl.when(pl.program_id(1) == 0)
    def _():
        carry_ref[...] = jnp.zeros_like(carry_ref)

    c = c + carry_ref[:, :1]
    carry_ref[...] = jnp.broadcast_to(c[:, tm - 1:tm], carry_ref.shape)
    c2 = c * LOG2E

    def parts(v):
        hi = v.astype(BF16).astype(F32)
        mid = (v - hi).astype(BF16).astype(F32)
        return [hi, mid, v - hi - mid]

    q_part_lane0 = BIAS_LANE0 + BIAS_PARTS * N_HEADS
    used = q_part_lane0 + BIAS_PARTS * N_HEADS
    shared = jnp.concatenate(
        [jnp.zeros((BIAS_LANE0, tm), F32)] + parts(-c2 - gb_shift_ref[:, :1]) + parts(c2)
        + [jnp.zeros((LANES - used, tm), F32)], axis=0).T
    in_k_parts = (lane >= BIAS_LANE0) & (lane < q_part_lane0)
    in_q_parts = (lane >= q_part_lane0) & (lane < q_part_lane0 + BIAS_PARTS * N_HEADS)
    k_tail = jnp.where(in_k_parts, shared, 0.0)
    q_tail = jnp.where(in_q_parts, shared, 0.0)
    lane_row = lane[:1]
    own = [(lane_row & (N_HEADS - 1)) == hd for hd in range(N_HEADS)]
    ones_q = [jnp.where(in_k_parts[:1] & o, 1.0, 0.0) for o in own]
    ones_k = [jnp.where(in_q_parts[:1] & o, 1.0, 0.0) for o in own]

    qb = proj(C_QB, C_KB)
    qbn = qb * lax.rsqrt(_group_mean_sq(qb, gb_mat_ref) + EPS) * qbw_ref[...]
    for hd, t in enumerate(_head_tiles(qbn)):
        qb_ref[0, :, hd * HEAD_PAD:(hd + 1) * HEAD_PAD] = jnp.where(
            head_lane, t, q_tail + ones_q[hd]).astype(BF16)
    kb = proj(C_KB, C_VB)
    kbn = kb * lax.rsqrt(_group_mean_sq(kb, gb_mat_ref) + EPS) * kbw_ref[...]
    for hd, t in enumerate(_head_tiles(kbn)):
        kb_ref[0, :, hd * HEAD_PAD:(hd + 1) * HEAD_PAD] = jnp.where(
            head_lane, t, k_tail + ones_k[hd]).astype(BF16)
    vbt_ref[0] = proj(C_VB, C_GA).T.astype(BF16)
    assert not pieces


def _in_proj(x, ada3, pos3, consts, tm):
    bsz, seq, d = x.shape
    grid = (bsz, seq // tm)
    tok = lambda width: pl.BlockSpec((1, tm, width), lambda b, i: (b, i, 0))
    tok_t = pl.BlockSpec((1, WIDTH, tm), lambda b, i: (b, 0, i))
    in_specs = [tok(d),
                pl.BlockSpec((1, 3, d), lambda b, i: (b, 0, 0)),
                pl.BlockSpec((1, 1, tm), lambda b, i: (b, 0, i))]
    in_specs += [_const_spec(a.shape) for a in consts]
    row_major = lambda w: jax.ShapeDtypeStruct((bsz, seq, w), BF16)
    transposed = jax.ShapeDtypeStruct((bsz, WIDTH, seq), BF16)
    out_shape = [row_major(QK_WIDTH), row_major(QK_WIDTH), transposed,
                 row_major(QK_WIDTH), row_major(QK_WIDTH), transposed,
                 row_major(WIDTH), row_major(WIDTH), row_major(d), row_major(d)]
    out_specs = [tok(QK_WIDTH), tok(QK_WIDTH), tok_t,
                 tok(QK_WIDTH), tok(QK_WIDTH), tok_t,
                 tok(WIDTH), tok(WIDTH), tok(d), tok(d)]
    return pl.pallas_call(
        _in_proj_kernel,
        grid=grid,
        in_specs=in_specs,
        out_specs=out_specs,
        out_shape=out_shape,
        scratch_shapes=[pltpu.VMEM((N_HEADS, LANES), F32)],
        compiler_params=pltpu.CompilerParams(
            dimension_semantics=("parallel", "arbitrary"),
            vmem_limit_bytes=VMEM_LIMIT),
        name="in_proj",
    )(x, ada3, pos3, *consts)


def _attn_kernel(flag_ref, qa_ref, ka_ref, vat_ref, ga_ref, qb_ref, kb_ref, vbt_ref,
                 gb_ref, oa_ref, ob_ref, *, tq, n_q_tiles):
    half = tq // 2
    nt = (((1,), (1,)), ((), ()))
    key = lax.broadcasted_iota(jnp.int32, (half, half), 0)
    qry = lax.broadcasted_iota(jnp.int32, (half, half), 1)
    ones = jnp.ones((BF16_ROWS, n_q_tiles * tq), BF16)
    branches = [(qa_ref, ka_ref, vat_ref, ga_ref, oa_ref, (key // CHUNK) <= (qry // CHUNK)),
                (qb_ref, kb_ref, vbt_ref, gb_ref, ob_ref, key <= qry)]
    chains = [(br, hh) for br in range(2) for hh in range(2)]

    def lanes(hh):
        return slice(hh * HEAD_PAD, (hh + 1) * HEAD_PAD)

    def values(br, hh, lo, hi):
        vt = branches[br][2][0, hh * V_DIM:(hh + 1) * V_DIM, lo:hi]
        return jnp.concatenate([vt, ones[:, :hi - lo]], axis=0)

    def probs(st, m, in_f32):
        if in_f32:
            return jnp.exp2(st).astype(BF16)
        return jnp.exp2((st - m).astype(BF16))

    def full_scores(br, hh, t, j):
        q_ref, k_ref = branches[br][:2]
        st = lax.dot_general(k_ref[0, j * tq:(j + 1) * tq, lanes(hh)],
                             q_ref[0, t * tq:(t + 1) * tq, lanes(hh)], nt,
                             preferred_element_type=F32)
        return st, jnp.max(st, axis=0, keepdims=True)

    def full_pv(br, hh, j, st, m, in_f32=False):
        return jnp.dot(values(br, hh, j * tq, (j + 1) * tq), probs(st, m, in_f32),
                       preferred_element_type=F32)

    def diag_scores(br, hh, t):
        q_ref, k_ref = branches[br][:2]
        mask = branches[br][5]
        base = t * tq
        st_lo = lax.dot_general(k_ref[0, base:base + half, lanes(hh)],
                                q_ref[0, base:base + tq, lanes(hh)], nt,
                                preferred_element_type=F32)
        st_hi = lax.dot_general(k_ref[0, base + half:base + tq, lanes(hh)],
                                q_ref[0, base + half:base + tq, lanes(hh)], nt,
                                preferred_element_type=F32)
        st_lo = jnp.concatenate(
            [jnp.where(mask, st_lo[:, :half], -jnp.inf), st_lo[:, half:]], axis=1)
        st_hi = jnp.where(mask, st_hi, -jnp.inf)
        mx_lo = jnp.max(st_lo, axis=0, keepdims=True)
        mx_hi = jnp.max(st_hi, axis=0, keepdims=True)
        mx = jnp.concatenate(
            [mx_lo[:, :half], jnp.maximum(mx_lo[:, half:], mx_hi)], axis=1)
        return (st_lo, st_hi), mx

    def diag_pv(br, hh, t, st, m, in_f32=False):
        st_lo, st_hi = st
        base = t * tq
        pv = jnp.dot(values(br, hh, base, base + half), probs(st_lo, m, in_f32),
                     preferred_element_type=F32)
        pv_hi = jnp.dot(values(br, hh, base + half, base + tq),
                        probs(st_hi, m[:, half:], in_f32), preferred_element_type=F32)
        return jnp.concatenate([pv[:, :half], pv[:, half:] + pv_hi], axis=1)

    def scores(br, hh, t, j):
        return diag_scores(br, hh, t) if j == t else full_scores(br, hh, t, j)

    def update(br, hh, t, j, st, mx, state):
        m_new = mx if state is None else jnp.maximum(state[0], mx)
        pv = diag_pv(br, hh, t, st, m_new) if j == t else full_pv(br, hh, j, st, m_new)
        if state is None:
            return m_new, pv
        m, acc = state
        return m_new, jnp.exp2(m - m_new) * acc + pv

    def finish(t, accs):
        for br in range(2):
            g_ref, o_ref = branches[br][3:5]
            halves = [accs[(br, hh)] for hh in range(2)]
            halves = [acc[:V_DIM] / acc[V_DIM:V_DIM + 1] for acc in halves]
            o = jnp.concatenate(halves, axis=0).T
            g = g_ref[0, t * tq:(t + 1) * tq, :].astype(F32)
            o_ref[0, t * tq:(t + 1) * tq, :] = (o * (g * jax.nn.sigmoid(g))).astype(BF16)

    def shifted_path():
        zero = jnp.zeros((1, tq), F32)

        def past_scores(br, hh, t):
            q_ref, k_ref = branches[br][:2]
            return lax.dot_general(k_ref[0, :t * tq, lanes(hh)],
                                   q_ref[0, t * tq:(t + 1) * tq, lanes(hh)], nt,
                                   preferred_element_type=F32)

        for t in range(n_q_tiles):
            diag = {c: diag_scores(*c, t)[0] for c in chains}
            past = {c: past_scores(*c, t) for c in chains} if t else None
            accs = {}
            for c in chains:
                acc = diag_pv(*c, t, diag[c], zero, in_f32=True)
                if t:
                    acc = acc + jnp.dot(values(*c, 0, t * tq), probs(past[c], zero, True),
                                        preferred_element_type=F32)
                accs[c] = acc
            finish(t, accs)

    def online_path():
        for t in range(n_q_tiles):
            state = {c: None for c in chains}
            nxt = {c: scores(*c, t, 0) for c in chains}
            for j in range(t + 1):
                cur = nxt
                if j < t:
                    nxt = {c: scores(*c, t, j + 1) for c in chains}
                state = {c: update(*c, t, j, *cur[c], state[c]) for c in chains}
            finish(t, {c: state[c][1] for c in chains})

    pl.when(flag_ref[0] != 0)(shifted_path)
    pl.when(flag_ref[0] == 0)(online_path)


def _attention(flag, qa, ka, vat, ga, qb, kb, vbt, gb, *, tq):
    bsz, seq, _ = qa.shape
    pairs = N_HEADS // 2
    kern = functools.partial(_attn_kernel, tq=tq, n_q_tiles=seq // tq)
    qk_spec = pl.BlockSpec((1, seq, 2 * HEAD_PAD), lambda b, p: (b, 0, p))
    vt_spec = pl.BlockSpec((1, 2 * V_DIM, seq), lambda b, p: (b, p, 0))
    tok_spec = pl.BlockSpec((1, seq, 2 * V_DIM), lambda b, p: (b, 0, p))
    out = jax.ShapeDtypeStruct((bsz, seq, WIDTH), BF16)
    return pl.pallas_call(
        kern,
        grid=(bsz, pairs),
        in_specs=[pl.BlockSpec(memory_space=pltpu.SMEM)]
        + 2 * [qk_spec, qk_spec, vt_spec, tok_spec],
        out_specs=[tok_spec, tok_spec],
        out_shape=[out, out],
        compiler_params=pltpu.CompilerParams(
            dimension_semantics=("parallel", "arbitrary"),
            vmem_limit_bytes=VMEM_LIMIT),
        name="attn",
    )(flag, qa, ka, vat, ga, qb, kb, vbt, gb)


def _out_kernel(x_ref, ada_ref, ya_ref, yb_ref, ma_ref, mb_ref, wa_ref, wb_ref,
                wo_ref, o_ref):
    for r in range(0, x_ref.shape[1], OUT_ROWS):
        rows = slice(r, r + OUT_ROWS)
        pa = jnp.dot(ya_ref[0, rows], wa_ref[...], preferred_element_type=F32)
        pb = jnp.dot(yb_ref[0, rows], wb_ref[...], preferred_element_type=F32)
        merged = (jax.nn.sigmoid(ma_ref[0, rows].astype(F32)) * pa
                  + jax.nn.sigmoid(mb_ref[0, rows].astype(F32)) * pb)
        upd = jnp.dot(merged.astype(BF16), wo_ref[...], preferred_element_type=F32)
        o_ref[0, rows] = x_ref[0, rows] + ada_ref[0, 2:3, :] * upd


def _out_proj(x, ada3, ya, yb, ma, mb, wa, wb, wo, tm):
    bsz, seq, d = x.shape
    tok = lambda width: pl.BlockSpec((1, tm, width), lambda b, i: (b, i, 0))
    return pl.pallas_call(
        _out_kernel,
        grid=(bsz, seq // tm),
        in_specs=[tok(d), pl.BlockSpec((1, 3, d), lambda b, i: (b, 0, 0)),
                  tok(WIDTH), tok(WIDTH), tok(d), tok(d),
                  _const_spec(wa.shape), _const_spec(wb.shape), _const_spec(wo.shape)],
        out_specs=tok(d),
        out_shape=jax.ShapeDtypeStruct((bsz, seq, d), x.dtype),
        compiler_params=pltpu.CompilerParams(
            dimension_semantics=("parallel", "arbitrary"),
            vmem_limit_bytes=VMEM_LIMIT),
        name="out_proj",
    )(x, ada3, ya, yb, ma, mb, wa, wb, wo)


def _tile_heads(vec, pad_to):
    v = jnp.pad(vec.astype(F32), (0, pad_to - vec.shape[0]))
    return jnp.tile(v, N_HEADS).reshape(1, N_HEADS * pad_to)


def _block_avg(size, total):
    gid = jnp.arange(total) // size
    return jnp.where(gid[:, None] == gid[None, :], 1.0 / size, 0.0).astype(BF16)


def _shift_parts(g):
    hi = (-g).astype(BF16).astype(F32)
    mid = (-g - hi).astype(BF16).astype(F32)
    return hi, mid, -g - hi - mid


def kernel(x, c, positions, w_ada, b_ada, norm_w, w_in, b_f, q_lora_norm_w,
           kv_lora_norm_w, w_uq, w_ukv, qn_nope_a, qn_rope_a, kn_nope_a,
           kn_rope_a, qn_b, kn_b, w_branch_a, w_branch_b, w_out):
    bsz, seq, d = x.shape
    depth = w_in.shape[0]
    tm = 512
    tq = 512
    pad = HEAD_PAD - NOPE_DIM - ROPE_DIM

    inv_freq = ROPE_THETA ** (-jnp.arange(0, ROPE_DIM, 2, dtype=F32) / ROPE_DIM)
    invf = inv_freq.reshape(ROPE_DIM // 2, 1)
    pos3 = positions.reshape(bsz, 1, seq)
    g_b = _block_avg(V_DIM, MXU_DIM)
    g_r = _block_avg(ROPE_DIM, MXU_DIM)
    sm_a = LOG2E / math.sqrt(NOPE_DIM + ROPE_DIM)
    sm_b = LOG2E / math.sqrt(V_DIM)

    for l in range(depth):
        ada3 = _ada_proj(c, w_ada[l], b_ada[l]).reshape(bsz, 3, d)

        wi = w_in[l]
        o = [0]
        for s in (Q_LORA, KV_LORA, ROPE_DIM, WIDTH, WIDTH, WIDTH, WIDTH, N_HEADS,
                  WIDTH, D_MODEL, D_MODEL):
            o.append(o[-1] + s)
        (w_cq, w_ckv, w_kr, w_ga, w_qb, w_kb, w_vb, w_f, w_gb, w_ma,
         w_mb) = [wi[:, o[n]:o[n + 1]] for n in range(11)]
        w_lat = jnp.concatenate(
            [w_cq, w_ckv, w_f, jnp.zeros((d, ROPE_LANE0 - N_HEADS), F32), w_kr,
             jnp.zeros((d, pad), F32)], axis=1)
        win = [w.astype(BF16) for w in (w_lat, w_qb, w_kb, w_vb, w_ga, w_gb, w_ma, w_mb)]

        wq = w_uq[l].reshape(Q_LORA, N_HEADS, NOPE_DIM + ROPE_DIM)
        wuqn = wq[:, :, :NOPE_DIM].reshape(Q_LORA, N_HEADS * NOPE_DIM).astype(BF16)
        wuqr = wq[:, :, NOPE_DIM:].reshape(Q_LORA, N_HEADS * ROPE_DIM).astype(BF16)
        wkv = w_ukv[l].reshape(KV_LORA, N_HEADS, NOPE_DIM + V_DIM)
        wuk = wkv[:, :, :NOPE_DIM].reshape(KV_LORA, N_HEADS * NOPE_DIM).astype(BF16)
        wuv = wkv[:, :, NOPE_DIM:].reshape(KV_LORA, WIDTH).astype(BF16)

        qanw = _tile_heads(qn_nope_a[l] * sm_a, NOPE_DIM)
        qarw = _tile_heads(qn_rope_a[l] * sm_a, ROPE_DIM)
        kaw = _tile_heads(kn_nope_a[l], NOPE_DIM)
        krw = jnp.concatenate([jnp.zeros((ROPE_LANE0,), F32), kn_rope_a[l],
                               jnp.zeros((pad,), F32)]).reshape(1, LANES)
        qbw = _tile_heads(qn_b[l] * sm_b, V_DIM)
        kbw = _tile_heads(kn_b[l], V_DIM)
        bfv = b_f[l].astype(F32).reshape(N_HEADS, 1)

        amax = lambda v: jnp.max(jnp.abs(v.astype(F32)))
        bound_a = sm_a * BOUND_SLACK * jnp.sqrt(
            (NOPE_DIM * amax(qn_nope_a[l]) ** 2 + ROPE_DIM * amax(qn_rope_a[l]) ** 2)
            * (NOPE_DIM * amax(kn_nope_a[l]) ** 2 + ROPE_DIM * amax(kn_rope_a[l]) ** 2))
        bound_b = sm_b * BOUND_SLACK * V_DIM * amax(qn_b[l]) * amax(kn_b[l])
        bounded = jnp.maximum(bound_a, bound_b) <= SHIFT_MAX
        shift_a = jnp.where(bounded, bound_a, 0.0)
        shift_b = jnp.where(bounded, bound_b, 0.0)
        ga_shift = jnp.zeros((LANES,), F32).at[SHIFT_LANE0:SHIFT_LANE0 + BIAS_PARTS].set(
            jnp.stack(_shift_parts(shift_a))).reshape(1, LANES)
        gb_shift = jnp.full((1, LANES), shift_b, F32)
        flag = bounded.astype(jnp.int32).reshape(1)

        consts = [norm_w[l].reshape(1, d), *win, wuqn, wuqr, wuk, wuv, g_b, g_r,
                  q_lora_norm_w[l].reshape(1, Q_LORA),
                  kv_lora_norm_w[l].reshape(1, KV_LORA), qanw, qarw, kaw, krw, qbw,
                  kbw, invf, bfv, ga_shift, gb_shift]
        (qa, ka, vat, qb, kb, vbt, ga, gb, ma, mb) = _in_proj(x, ada3, pos3, consts, tm)

        ya, yb = _attention(flag, qa, ka, vat, ga, qb, kb, vbt, gb, tq=tq)

        x = _out_proj(x, ada3, ya, yb, ma, mb, w_branch_a[l].astype(BF16),
                      w_branch_b[l].astype(BF16), w_out[l].astype(BF16), 2 * tm)
    return x
```
